```python
import jax, jax.numpy as jnp
from jax import lax
import numpy as np

D_MODEL = 2048
BATCH = 8
SEQ = 4096
DEPTH = 4
DEC_BATCH = 8
DEC_SEQ = 16
PAST_LEN = 2048

CHUNK = 64
Q_BLOCK = 128
N_MIXERS = 3
N_LAYERS_A = (DEPTH + 2) // 3
N_LAYERS_B = (DEPTH + 1) // 3
N_LAYERS_C = DEPTH // 3
A_HEADS = 16
A_HEAD_DIM = D_MODEL // A_HEADS
A_WIDTH = A_HEADS * A_HEAD_DIM
FORGET_BIAS_INIT = 3.0
B_HEADS = 16
B_HEAD_DIM = D_MODEL // B_HEADS
B_WIDTH = B_HEADS * B_HEAD_DIM
C_Q_HEADS = 32
C_KV_HEADS = 4
C_GROUP = C_Q_HEADS // C_KV_HEADS
C_HEAD_DIM = D_MODEL // C_Q_HEADS
C_Q_WIDTH = C_Q_HEADS * C_HEAD_DIM
C_KV_WIDTH = C_KV_HEADS * C_HEAD_DIM
C_WINDOW = 128
C_WINDOW_CHUNKS = C_WINDOW // CHUNK
ROPE_THETA = 500000.0
ROPE_DIM = C_HEAD_DIM // 4
D_FF = 11 * D_MODEL // 4
FFN_RES = 0.5
NORM_EPS = 1e-6

kernel_name = "hybrid_stream_fox_sb_swa_macaron_step"


def rms_norm(x, g):
    xf = x.astype(jnp.float32)
    y = xf * lax.rsqrt(jnp.mean(xf * xf, axis=-1, keepdims=True) + NORM_EPS)
    return y.astype(x.dtype) * g


def half_ffn(x, g_pre, g_post, w_in, w_out):
    gu = rms_norm(x, g_pre) @ w_in
    a = jax.nn.silu(gu[..., :D_FF]) * gu[..., D_FF:]
    return x + FFN_RES * rms_norm(a @ w_out, g_post)


def rope_partial(x, pos):
    half = ROPE_DIM // 2
    inv = ROPE_THETA ** (-jnp.arange(half, dtype=jnp.float32) / half)
    ang = pos.astype(jnp.float32)[:, None] * inv[None, :]
    cos = jnp.cos(ang)[None, :, None, :]
    sin = jnp.sin(ang)[None, :, None, :]
    xr = x[..., :ROPE_DIM].astype(jnp.float32)
    x1, x2 = xr[..., :half], xr[..., half:]
    rot = jnp.concatenate([x1 * cos - x2 * sin, x2 * cos + x1 * sin], axis=-1).astype(x.dtype)
    return jnp.concatenate([rot, x[..., ROPE_DIM:]], axis=-1)


def to_blocks(a):
    b, s = a.shape[:2]
    return jnp.moveaxis(a.reshape((b, s // Q_BLOCK, Q_BLOCK) + a.shape[2:]), 1, 0)


def from_blocks(o):
    nb, b, q = o.shape[:3]
    return jnp.moveaxis(o, 0, 1).reshape(b, nb * q, -1)


def fox_project(h, w_in, b_f):
    b, s, _ = h.shape
    proj = h @ w_in
    qkv = proj[..., :3 * A_WIDTH].reshape(b, s, 3, A_HEADS, A_HEAD_DIM)
    logf = jax.nn.log_sigmoid((proj[..., 3 * A_WIDTH:] + b_f).astype(jnp.float32))
    return qkv[:, :, 0], qkv[:, :, 1], qkv[:, :, 2], logf


def fox_attend(q, cq, k, v, ck, qpos, kpos):
    s = jnp.einsum('bqhd,bkhd->bhqk', q, k).astype(jnp.float32) * (A_HEAD_DIM ** -0.5)
    s = s + jnp.swapaxes(cq, 1, 2)[..., :, None] - jnp.swapaxes(ck, 1, 2)[..., None, :]
    s = jnp.where(kpos[None, :] <= qpos[:, None], s, -jnp.inf)
    p = jax.nn.softmax(s, axis=-1).astype(v.dtype)
    return jnp.einsum('bhqk,bkhd->bqhd', p, v)


def fox_prompt(h, w_in, b_f, w_out):
    q, k, v, logf = fox_project(h, w_in, b_f)
    s_len = h.shape[1]
    cum = jnp.cumsum(logf, axis=1)
    kpos = jnp.arange(s_len)

    def blk(args):
        i, qi, ci = args
        return fox_attend(qi, ci, k, v, cum, i * Q_BLOCK + jnp.arange(Q_BLOCK), kpos)

    o = lax.map(blk, (jnp.arange(s_len // Q_BLOCK), to_blocks(q), to_blocks(cum)))
    return from_blocks(o) @ w_out, k, v, logf.astype(h.dtype)


def fox_sample(h, cache_k, cache_v, cache_logf, w_in, b_f, w_out):
    q, k, v, logf = fox_project(h, w_in, b_f)
    b, t, _ = h.shape
    p_len = cache_k.shape[1]
    kk = jnp.concatenate([cache_k, k], axis=1)
    vv = jnp.concatenate([cache_v, v], axis=1)
    cum = jnp.cumsum(jnp.concatenate([cache_logf.astype(jnp.float32), logf], axis=1), axis=1)
    o = fox_attend(q, cum[:, p_len:], kk, vv, cum, p_len + jnp.arange(t), jnp.arange(p_len + t))
    return o.reshape(b, t, -1) @ w_out, k, v, logf.astype(h.dtype)


def sb_project(h, w_in):
    b, s, _ = h.shape
    qkv = (h @ w_in).reshape(b, s, 3, B_HEADS, B_HEAD_DIM)
    return qkv[:, :, 0], qkv[:, :, 1], qkv[:, :, 2]


def sb_attend(q, k, v, qpos, kpos):
    z = jnp.einsum('bqhd,bkhd->bhqk', q, k).astype(jnp.float32) * (B_HEAD_DIM ** -0.5)
    before = kpos[None, :] < qpos[:, None]
    log_fail = jnp.where(before, jax.nn.log_sigmoid(-z), 0.0)
    later = lax.cumsum(log_fail, axis=3, reverse=True) - log_fail
    w = jnp.where(before, jnp.exp(jax.nn.log_sigmoid(z) + later), 0.0)
    return jnp.einsum('bhqk,bkhd->bqhd', w.astype(v.dtype), v)


def sb_prompt(h, w_in, w_out):
    q, k, v = sb_project(h, w_in)
    s_len = h.shape[1]
    kpos = jnp.arange(s_len)

    def blk(args):
        i, qi = args
        return sb_attend(qi, k, v, i * Q_BLOCK + jnp.arange(Q_BLOCK), kpos)

    o = lax.map(blk, (jnp.arange(s_len // Q_BLOCK), to_blocks(q)))
    return from_blocks(o) @ w_out, k, v


def sb_sample(h, cache_k, cache_v, w_in, w_out):
    q, k, v = sb_project(h, w_in)
    b, t, _ = h.shape
    p_len = cache_k.shape[1]
    kk = jnp.concatenate([cache_k, k], axis=1)
    vv = jnp.concatenate([cache_v, v], axis=1)
    o = sb_attend(q, kk, vv, p_len + jnp.arange(t), jnp.arange(p_len + t))
    return o.reshape(b, t, -1) @ w_out, k, v


def swa_project(h, w_in, pos):
    b, s, _ = h.shape
    proj = h @ w_in
    q = proj[..., :C_Q_WIDTH].reshape(b, s, C_Q_HEADS, C_HEAD_DIM)
    k = proj[..., C_Q_WIDTH:C_Q_WIDTH + C_KV_WIDTH].reshape(b, s, C_KV_HEADS, C_HEAD_DIM)
    v = proj[..., C_Q_WIDTH + C_KV_WIDTH:].reshape(b, s, C_KV_HEADS, C_HEAD_DIM)
    return rope_partial(q, pos), rope_partial(k, pos), v


def swa_attend(qg, k, v, qpos, kpos, sink_b):
    s = jnp.einsum('bqhgd,bkhd->bhgqk', qg, k).astype(jnp.float32) * (C_HEAD_DIM ** -0.5)
    dist = qpos[:, None] // CHUNK - kpos[None, :] // CHUNK
    vis = (kpos[None, :] >= 0) & (dist >= 0) & (dist <= C_WINDOW_CHUNKS)
    s = jnp.where(vis, s, -jnp.inf)
    m = jnp.maximum(jnp.max(s, axis=-1, keepdims=True), sink_b)
    e = jnp.exp(s - m)
    p = e / (jnp.sum(e, axis=-1, keepdims=True) + jnp.exp(sink_b - m))
    return jnp.einsum('bhgqk,bkhd->bqhgd', p.astype(v.dtype), v)


def swa_prompt(h, w_in, sink, w_out):
    b, s_len, _ = h.shape
    q, k, v = swa_project(h, w_in, jnp.arange(s_len))
    pad = C_WINDOW_CHUNKS * CHUNK
    band = pad + CHUNK
    kp = jnp.pad(k, ((0, 0), (pad, 0), (0, 0), (0, 0)))
    vp = jnp.pad(v, ((0, 0), (pad, 0), (0, 0), (0, 0)))
    nc = s_len // CHUNK
    qc = jnp.moveaxis(q.reshape(b, nc, CHUNK, C_KV_HEADS, C_GROUP, C_HEAD_DIM), 1, 0)
    sink_b = sink.astype(jnp.float32).reshape(C_KV_HEADS, C_GROUP)[None, :, :, None, None]

    def blk(args):
        c, qi = args
        start = c * CHUNK
        kb = lax.dynamic_slice_in_dim(kp, start, band, axis=1)
        vb = lax.dynamic_slice_in_dim(vp, start, band, axis=1)
        return swa_attend(qi, kb, vb, start + jnp.arange(CHUNK), start - pad + jnp.arange(band), sink_b)

    o = lax.map(blk, (jnp.arange(nc), qc))
    o = jnp.moveaxis(o, 0, 1).reshape(b, s_len, -1)
    buf = min(C_WINDOW, s_len)
    return o @ w_out, k[:, s_len - buf:], v[:, s_len - buf:]


def swa_sample(h, cache_k, cache_v, p_len, w_in, sink, w_out):
    b, t, _ = h.shape
    q, k, v = swa_project(h, w_in, p_len + jnp.arange(t))
    buf = cache_k.shape[1]
    kk = jnp.concatenate([cache_k, k], axis=1)
    vv = jnp.concatenate([cache_v, v], axis=1)
    sink_b = sink.astype(jnp.float32).reshape(C_KV_HEADS, C_GROUP)[None, :, :, None, None]
    qg = q.reshape(b, t, C_KV_HEADS, C_GROUP, C_HEAD_DIM)
    o = swa_attend(qg, kk, vv, p_len + jnp.arange(t), jnp.arange(p_len - buf, p_len + t), sink_b)
    return o.reshape(b, t, -1) @ w_out, kk[:, -buf:], vv[:, -buf:]


def setup_inputs(seed: int = 0) -> dict:
    key = jax.random.key(seed)
    ks = jax.random.split(key, 20)
    f32 = jnp.float32

    def nrm(k, shape, scale=1.0):
        return jax.random.normal(k, shape, f32) * scale

    c_buf = min(C_WINDOW, PAST_LEN)
    return {
        "x_prompt": nrm(ks[0], (BATCH, SEQ, D_MODEL)),
        "x_sample": nrm(ks[1], (DEC_BATCH, DEC_SEQ, D_MODEL)),
        "cache_a_k": nrm(ks[2], (N_LAYERS_A, DEC_BATCH, PAST_LEN, A_HEADS, A_HEAD_DIM)),
        "cache_a_v": nrm(ks[3], (N_LAYERS_A, DEC_BATCH, PAST_LEN, A_HEADS, A_HEAD_DIM)),
        "cache_a_logf": jax.nn.log_sigmoid(FORGET_BIAS_INIT + nrm(ks[4], (N_LAYERS_A, DEC_BATCH, PAST_LEN, A_HEADS), 0.5)),
        "cache_b_k": nrm(ks[5], (N_LAYERS_B, DEC_BATCH, PAST_LEN, B_HEADS, B_HEAD_DIM)),
        "cache_b_v": nrm(ks[6], (N_LAYERS_B, DEC_BATCH, PAST_LEN, B_HEADS, B_HEAD_DIM)),
        "cache_c_k": nrm(ks[7], (N_LAYERS_C, DEC_BATCH, c_buf, C_KV_HEADS, C_HEAD_DIM)),
        "cache_c_v": nrm(ks[8], (N_LAYERS_C, DEC_BATCH, c_buf, C_KV_HEADS, C_HEAD_DIM)),
        "norm_g": 1.0 + nrm(ks[9], (DEPTH, 6, D_MODEL), 0.05),
        "ffn_w_in": nrm(ks[10], (DEPTH, 2, D_MODEL, 2 * D_FF), D_MODEL ** -0.5),
        "ffn_w_out": nrm(ks[11], (DEPTH, 2, D_FF, D_MODEL), D_FF ** -0.5),
        "a_w_in": nrm(ks[12], (N_LAYERS_A, D_MODEL, 3 * A_WIDTH + A_HEADS), D_MODEL ** -0.5),
        "a_b_f": FORGET_BIAS_INIT + nrm(ks[13], (N_LAYERS_A, A_HEADS), 0.5),
        "a_w_out": nrm(ks[14], (N_LAYERS_A, A_WIDTH, D_MODEL), A_WIDTH ** -0.5),
        "b_w_in": nrm(ks[15], (N_LAYERS_B, D_MODEL, 3 * B_WIDTH), D_MODEL ** -0.5),
        "b_w_out": nrm(ks[16], (N_LAYERS_B, B_WIDTH, D_MODEL), B_WIDTH ** -0.5),
        "c_w_in": nrm(ks[17], (N_LAYERS_C, D_MODEL, C_Q_WIDTH + 2 * C_KV_WIDTH), D_MODEL ** -0.5),
        "c_sink": nrm(ks[18], (N_LAYERS_C, C_Q_HEADS), 0.5),
        "c_w_out": nrm(ks[19], (N_LAYERS_C, C_Q_WIDTH, D_MODEL), C_Q_WIDTH ** -0.5),
    }


def reference(x_prompt, x_sample, cache_a_k, cache_a_v, cache_a_logf, cache_b_k, cache_b_v,
              cache_c_k, cache_c_v, norm_g, ffn_w_in, ffn_w_out, a_w_in, a_b_f, a_w_out,
              b_w_in, b_w_out, c_w_in, c_sink, c_w_out):
    xp, xs = x_prompt, x_sample
    p_len = cache_a_k.shape[2]
    a_k_p, a_v_p, a_lf_p, b_k_p, b_v_p, c_k_p, c_v_p = [], [], [], [], [], [], []
    a_k_s, a_v_s, a_lf_s, b_k_s, b_v_s, c_k_s, c_v_s = [], [], [], [], [], [], []
    for layer in range(DEPTH):
        g = norm_g[layer]
        xp = half_ffn(xp, g[0], g[1], ffn_w_in[layer, 0], ffn_w_out[layer, 0])
        xs = half_ffn(xs, g[0], g[1], ffn_w_in[layer, 0], ffn_w_out[layer, 0])
        hp = rms_norm(xp, g[2])
        hs = rms_norm(xs, g[2])
        kind = layer % N_MIXERS
        j = layer // N_MIXERS
        if kind == 0:
            mp, kp_, vp_, lp_ = fox_prompt(hp, a_w_in[j], a_b_f[j], a_w_out[j])
            ms, ks_, vs_, ls_ = fox_sample(hs, cache_a_k[j], cache_a_v[j], cache_a_logf[j], a_w_in[j], a_b_f[j], a_w_out[j])
            a_k_p.append(kp_); a_v_p.append(vp_); a_lf_p.append(lp_)
            a_k_s.append(ks_); a_v_s.append(vs_); a_lf_s.append(ls_)
        elif kind == 1:
            mp, kp_, vp_ = sb_prompt(hp, b_w_in[j], b_w_out[j])
            ms, ks_, vs_ = sb_sample(hs, cache_b_k[j], cache_b_v[j], b_w_in[j], b_w_out[j])
            b_k_p.append(kp_); b_v_p.append(vp_)
            b_k_s.append(ks_); b_v_s.append(vs_)
        else:
            mp, kp_, vp_ = swa_prompt(hp, c_w_in[j], c_sink[j], c_w_out[j])
            ms, ks_, vs_ = swa_sample(hs, cache_c_k[j], cache_c_v[j], p_len, c_w_in[j], c_sink[j], c_w_out[j])
            c_k_p.append(kp_); c_v_p.append(vp_)
            c_k_s.append(ks_); c_v_s.append(vs_)
        xp = xp + rms_norm(mp, g[3])
        xs = xs + rms_norm(ms, g[3])
        xp = half_ffn(xp, g[4], g[5], ffn_w_in[layer, 1], ffn_w_out[layer, 1])
        xs = half_ffn(xs, g[4], g[5], ffn_w_in[layer, 1], ffn_w_out[layer, 1])
    return (xp, xs,
            jnp.stack(a_k_p), jnp.stack(a_v_p), jnp.stack(a_lf_p),
            jnp.stack(b_k_p), jnp.stack(b_v_p),
            jnp.stack(c_k_p), jnp.stack(c_v_p),
            jnp.stack(a_k_s), jnp.stack(a_v_s), jnp.stack(a_lf_s),
            jnp.stack(b_k_s), jnp.stack(b_v_s),
            jnp.stack(c_k_s), jnp.stack(c_v_s))
```

```python
import functools
import math

import jax
import jax.numpy as jnp
import numpy as np
from jax import lax
from jax.experimental import pallas as pl
from jax.experimental.pallas import tpu as pltpu

F32 = jnp.float32
BF16 = jnp.bfloat16

D_MODEL = 2048
D_FF = 11 * D_MODEL // 4
HEAD_DIM = 128
N_HEADS = 16
C_Q_HEADS = 32
C_KV_HEADS = 4
C_GROUP = C_Q_HEADS // C_KV_HEADS
C_HEAD_DIM = 64
C_WINDOW = 128
CHUNK = 64
C_WINDOW_CHUNKS = C_WINDOW // CHUNK
ROPE_THETA = 500000.0
ROPE_DIM = C_HEAD_DIM // 4
FFN_RES = 0.5
NORM_EPS = 1e-6
LANES = 128
VMEM_LIMIT = 56 * 1024 * 1024


def _cparams(n_grid):
    return pltpu.CompilerParams(dimension_semantics=("arbitrary",) * n_grid,
                                vmem_limit_bytes=VMEM_LIMIT)


def _rms(x, g):
    ms = jnp.mean(x * x, axis=-1, keepdims=True)
    return (x * lax.rsqrt(ms + NORM_EPS)) * g


def _dot(a, b):
    return jnp.dot(a, b, preferred_element_type=F32)


def _dot_nt(a, b):
    return lax.dot_general(a, b, (((1,), (1,)), ((), ())), preferred_element_type=F32)


def _ffn_kernel(x_ref, gpre_ref, gpost_ref, wg_ref, wu_ref, wo_ref, o_ref, xn_ref, acc_ref):
    j = pl.program_id(1)

    @pl.when(j == 0)
    def _():
        xn_ref[...] = _rms(x_ref[...], gpre_ref[...]).astype(BF16)
        acc_ref[...] = jnp.zeros_like(acc_ref)

    xn = xn_ref[...]
    gate = _dot(xn, wg_ref[...])
    up = _dot(xn, wu_ref[...])
    a = (gate * jax.nn.sigmoid(gate)) * up
    acc_ref[...] += _dot(a.astype(BF16), wo_ref[...])

    @pl.when(j == pl.num_programs(1) - 1)
    def _():
        o_ref[...] = x_ref[...] + FFN_RES * _rms(acc_ref[...], gpost_ref[...])


def _half_ffn(x, g_pre, g_post, w_in, w_out, tm, tf):
    t = x.shape[0]
    nf = D_FF // tf
    return pl.pallas_call(
        _ffn_kernel,
        grid=(t // tm, nf),
        in_specs=[
            pl.BlockSpec((tm, D_MODEL), lambda i, j: (i, 0)),
            pl.BlockSpec((1, D_MODEL), lambda i, j: (0, 0)),
            pl.BlockSpec((1, D_MODEL), lambda i, j: (0, 0)),
            pl.BlockSpec((D_MODEL, tf), lambda i, j: (0, j)),
            pl.BlockSpec((D_MODEL, tf), lambda i, j: (0, j + nf)),
            pl.BlockSpec((tf, D_MODEL), lambda i, j: (j, 0)),
        ],
        out_specs=pl.BlockSpec((tm, D_MODEL), lambda i, j: (i, 0)),
        out_shape=jax.ShapeDtypeStruct((t, D_MODEL), F32),
        scratch_shapes=[pltpu.VMEM((tm, D_MODEL), BF16), pltpu.VMEM((tm, D_MODEL), F32)],
        compiler_params=_cparams(2),
        name="half_ffn",
    )(x, g_pre.reshape(1, D_MODEL), g_post.reshape(1, D_MODEL), w_in, w_in, w_out)


def _log_sigmoid(x):
    return jnp.minimum(x, 0.0) - jnp.log1p(jnp.exp(-jnp.abs(x)))


def _proj_ab_kernel(*refs, with_forget):
    if with_forget:
        x_ref, g_ref, w_ref, wf_ref, bf_ref, q_ref, k_ref, v_ref, lf_ref, xn_ref = refs
    else:
        x_ref, g_ref, w_ref, q_ref, k_ref, v_ref, xn_ref = refs
    j = pl.program_id(1)

    @pl.when(j == 0)
    def _():
        xn = _rms(x_ref[...], g_ref[...]).astype(BF16)
        xn_ref[...] = xn
        if with_forget:
            f = _dot(xn, wf_ref[...]) + bf_ref[...]
            lf_ref[...] = _log_sigmoid(f)[:, :N_HEADS]

    y = _dot(xn_ref[...], w_ref[0])

    @pl.when(j == 0)
    def _():
        q_ref[...] = y.astype(BF16)

    @pl.when(j == 1)
    def _():
        k_ref[...] = y

    @pl.when(j == 2)
    def _():
        v_ref[...] = y


def _proj_ab(x, g, w3, wf, bf, tm):
    t = x.shape[0]
    with_forget = wf is not None
    row = pl.BlockSpec((tm, D_MODEL), lambda i, j: (i, 0))
    in_specs = [row, pl.BlockSpec((1, D_MODEL), lambda i, j: (0, 0)),
                pl.BlockSpec((1, D_MODEL, D_MODEL), lambda i, j: (j, 0, 0))]
    args = [x, g.reshape(1, D_MODEL), w3]
    out_specs = [row, row, row]
    out_shape = [jax.ShapeDtypeStruct((t, D_MODEL), BF16),
                 jax.ShapeDtypeStruct((t, D_MODEL), F32),
                 jax.ShapeDtypeStruct((t, D_MODEL), F32)]
    if with_forget:
        in_specs += [pl.BlockSpec((D_MODEL, LANES), lambda i, j: (0, 0)),
                     pl.BlockSpec((1, LANES), lambda i, j: (0, 0))]
        args += [wf, bf]
        out_specs.append(pl.BlockSpec((tm, N_HEADS), lambda i, j: (i, 0)))
        out_shape.append(jax.ShapeDtypeStruct((t, N_HEADS), F32))
    return pl.pallas_call(
        functools.partial(_proj_ab_kernel, with_forget=with_forget),
        grid=(t // tm, 3),
        in_specs=in_specs,
        out_specs=out_specs,
        out_shape=out_shape,
        scratch_shapes=[pltpu.VMEM((tm, D_MODEL), BF16)],
        compiler_params=_cparams(2),
        name="proj_a" if with_forget else "proj_b",
    )(*args)


def _rope_cols(y, cos, sa, sb):
    outs = []
    for c in range(y.shape[1] // LANES):
        blk = y[:, c * LANES:(c + 1) * LANES]
        nxt = pltpu.roll(blk, LANES - ROPE_DIM // 2, axis=1)
        prv = pltpu.roll(blk, ROPE_DIM // 2, axis=1)
        outs.append(blk * cos + nxt * sa + prv * sb)
    return outs


def _proj_c_kernel(x_ref, g_ref, wq_ref, wk_ref, wv_ref, cos_ref, sa_ref, sb_ref,
                   q_ref, k_ref, v_ref):
    xn = _rms(x_ref[...], g_ref[...]).astype(BF16)
    cos, sa, sb = cos_ref[...], sa_ref[...], sb_ref[...]
    q = _dot(xn, wq_ref[...])
    for c, blk in enumerate(_rope_cols(q, cos, sa, sb)):
        q_ref[:, c * LANES:(c + 1) * LANES] = blk.astype(BF16)
    k = _dot(xn, wk_ref[...])
    for c, blk in enumerate(_rope_cols(k, cos, sa, sb)):
        k_ref[:, c * LANES:(c + 1) * LANES] = blk
    v_ref[...] = _dot(xn, wv_ref[...])


def _proj_c(x, g, wq, wk2, wv2, cos, sa, sb, tm, rows_per_seq):
    t = x.shape[0]
    kvw = C_KV_HEADS * LANES
    nseq = rows_per_seq // tm
    row = lambda w: pl.BlockSpec((tm, w), lambda i: (i, 0))
    const = lambda a: pl.BlockSpec(a.shape, lambda i: (0, 0))
    tab = pl.BlockSpec((tm, LANES), lambda i: (i % nseq, 0))
    return pl.pallas_call(
        _proj_c_kernel,
        grid=(t // tm,),
        in_specs=[row(D_MODEL), pl.BlockSpec((1, D_MODEL), lambda i: (0, 0)),
                  const(wq), const(wk2), const(wv2), tab, tab, tab],
        out_specs=[row(D_MODEL), row(kvw), row(kvw)],
        out_shape=[jax.ShapeDtypeStruct((t, D_MODEL), BF16),
                   jax.ShapeDtypeStruct((t, kvw), F32),
                   jax.ShapeDtypeStruct((t, kvw), F32)],
        compiler_params=_cparams(1),
        name="proj_c",
    )(x, g.reshape(1, D_MODEL), wq, wk2, wv2, cos, sa, sb)


def _post_kernel(x_ref, o_ref, w_ref, g_ref, out_ref):
    y = _dot(o_ref[...], w_ref[...])
    out_ref[...] = x_ref[...] + _rms(y, g_ref[...])


def _post(x, o, w, g, tm):
    t = x.shape[0]
    row = pl.BlockSpec((tm, D_MODEL), lambda i: (i, 0))
    return pl.pallas_call(
        _post_kernel,
        grid=(t // tm,),
        in_specs=[row, row, pl.BlockSpec((D_MODEL, D_MODEL), lambda i: (0, 0)),
                  pl.BlockSpec((1, D_MODEL), lambda i: (0, 0))],
        out_specs=row,
        out_shape=jax.ShapeDtypeStruct((t, D_MODEL), F32),
        compiler_params=_cparams(1),
        name="mixer_out",
    )(x, o, w, g.reshape(1, D_MODEL))


def _cumsum_kernel(x_ref, o_ref):
    rows, length = x_ref.shape
    r = lax.broadcasted_iota(jnp.int32, (LANES, LANES), 0)
    c = lax.broadcasted_iota(jnp.int32, (LANES, LANES), 1)
    upper = (r <= c).astype(F32).astype(BF16)
    carry = jnp.zeros((rows, 1), F32)
    for b in range(length // LANES):
        x = x_ref[:, b * LANES:(b + 1) * LANES]
        hi = x.astype(BF16)
        r1 = x - hi.astype(F32)
        mid = r1.astype(BF16)
        lo = (r1 - mid.astype(F32)).astype(BF16)
        cs = _dot(hi, upper) + _dot(mid, upper) + _dot(lo, upper)
        o_ref[:, b * LANES:(b + 1) * LANES] = cs + carry
        carry = carry + cs[:, LANES - 1:LANES]


def _cumsum_lanes(x):
    return pl.pallas_call(
        _cumsum_kernel,
        out_shape=jax.ShapeDtypeStruct(x.shape, F32),
        compiler_params=pltpu.CompilerParams(vmem_limit_bytes=VMEM_LIMIT),
        name="cumsum",
    )(x)


def _fox_kernel(*refs, tq, tk, has_tail):
    if has_tail:
        q_ref, k_ref, v_ref, c_ref, kt_ref, vt_ref, ct_ref, o_ref, kb_ref, vb_ref = refs
    else:
        q_ref, k_ref, v_ref, c_ref, o_ref, kb_ref, vb_ref = refs
    sq = q_ref.shape[1]
    sm = k_ref.shape[1]
    scale = HEAD_DIM ** -0.5
    kb_ref[...] = k_ref[0].astype(BF16)
    vb_ref[...] = v_ref[0].astype(BF16)

    def step(carry, q, k, v, crow, mask):
        m, l, acc = carry
        s = _dot_nt(q, k) * scale - crow
        if mask is not None:
            s = jnp.where(mask, s, -jnp.inf)
        m_new = jnp.maximum(m, jnp.max(s, axis=1, keepdims=True))
        alpha = jnp.exp(m - m_new)
        p = jnp.exp(s - m_new)
        l = alpha * l + jnp.sum(p, axis=1, keepdims=True)
        acc = alpha * acc + _dot(p.astype(BF16), v)
        return m_new, l, acc

    def q_tile(qi, _):
        q0 = pl.multiple_of(qi * tq, tq)
        q = q_ref[0, pl.ds(q0, tq), :]
        init = (jnp.full((tq, 1), -jnp.inf, F32), jnp.zeros((tq, 1), F32),
                jnp.zeros((tq, HEAD_DIM), F32))

        def full_block(j, carry):
            k0 = pl.multiple_of(j * tk, tk)
            return step(carry, q, kb_ref[pl.ds(k0, tk), :], vb_ref[pl.ds(k0, tk), :],
                        c_ref[0, 0, :, pl.ds(k0, tk)], None)

        n_full = sm // tk if has_tail else qi
        carry = lax.fori_loop(0, n_full, full_block, init)
        if has_tail:
            nt = kt_ref.shape[1]
            row = lax.broadcasted_iota(jnp.int32, (tq, nt), 0)
            col = lax.broadcasted_iota(jnp.int32, (tq, nt), 1)
            carry = step(carry, q, kt_ref[0].astype(BF16), vt_ref[0].astype(BF16),
                         ct_ref[0, 0], col <= row)
        else:
            row = lax.broadcasted_iota(jnp.int32, (tq, tk), 0)
            col = lax.broadcasted_iota(jnp.int32, (tq, tk), 1)
            carry = step(carry, q, kb_ref[pl.ds(q0, tk), :], vb_ref[pl.ds(q0, tk), :],
                         c_ref[0, 0, :, pl.ds(q0, tk)], col <= row)
        _, l, acc = carry
        o_ref[0, pl.ds(q0, tq), :] = (acc / l).astype(BF16)
        return 0

    lax.fori_loop(0, sq // tq, q_tile, 0)


def _fox_attention(q, k, v, c, tail, tq, tk):
    b, sq, _ = q.shape
    sm = k.shape[1]
    has_tail = tail is not None
    head = lambda rows: pl.BlockSpec((1, rows, HEAD_DIM), lambda i, h: (i, 0, h))
    crow = lambda n: pl.BlockSpec((1, 1, 1, n), lambda i, h: (i, h, 0, 0))
    in_specs = [head(sq), head(sm), head(sm), crow(sm)]
    args = [q, k, v, c]
    if has_tail:
        in_specs += [head(sq), head(sq), crow(sq)]
        args += list(tail)
    return pl.pallas_call(
        functools.partial(_fox_kernel, tq=tq, tk=tk, has_tail=has_tail),
        grid=(b, N_HEADS),
        in_specs=in_specs,
        out_specs=head(sq),
        out_shape=jax.ShapeDtypeStruct((b, sq, D_MODEL), BF16),
        scratch_shapes=[pltpu.VMEM((sm, HEAD_DIM), BF16), pltpu.VMEM((sm, HEAD_DIM), BF16)],
        compiler_params=_cparams(2),
        name="fox_tail" if has_tail else "fox",
    )(*args)


def _sb_kernel(*refs, tq, tk, has_tail):
    if has_tail:
        q_ref, k_ref, v_ref, kt_ref, vt_ref, o_ref, kb_ref, vb_ref = refs
    else:
        q_ref, k_ref, v_ref, o_ref, kb_ref, vb_ref = refs
    sq = q_ref.shape[1]
    sm = k_ref.shape[1]
    scale = HEAD_DIM ** -0.5
    kb_ref[...] = k_ref[0].astype(BF16)
    vb_ref[...] = v_ref[0].astype(BF16)

    def suffix_matrix(n):
        r = lax.broadcasted_iota(jnp.int32, (n, n), 0)
        c = lax.broadcasted_iota(jnp.int32, (n, n), 1)
        return (r > c).astype(F32).astype(BF16)

    def step(carry, q, k, v, before, suffix):
        run, acc = carry
        z = _dot_nt(q, k) * scale
        sp = jnp.maximum(z, 0.0) + jnp.log1p(jnp.exp(-jnp.abs(z)))
        log_fail = -sp
        if before is not None:
            log_fail = jnp.where(before, log_fail, 0.0)
        hi = log_fail.astype(BF16)
        lo = (log_fail - hi.astype(F32)).astype(BF16)
        later = _dot(hi, suffix) + _dot(lo, suffix) + run
        w = jnp.exp((z - sp) + later)
        if before is not None:
            w = jnp.where(before, w, 0.0)
        acc = acc + _dot(w.astype(BF16), v)
        run = run + jnp.sum(log_fail, axis=1, keepdims=True)
        return run, acc

    def q_tile(qi, _):
        q0 = pl.multiple_of(qi * tq, tq)
        q = q_ref[0, pl.ds(q0, tq), :]
        carry = (jnp.zeros((tq, 1), F32), jnp.zeros((tq, HEAD_DIM), F32))
        if has_tail:
            nt = kt_ref.shape[1]
            row = lax.broadcasted_iota(jnp.int32, (tq, nt), 0)
            col = lax.broadcasted_iota(jnp.int32, (tq, nt), 1)
            carry = step(carry, q, kt_ref[0].astype(BF16), vt_ref[0].astype(BF16),
                         col < row, suffix_matrix(nt))
            n_full = sm // tk
        else:
            for d in reversed(range(tq // tk)):
                row = lax.broadcasted_iota(jnp.int32, (tq, tk), 0)
                col = lax.broadcasted_iota(jnp.int32, (tq, tk), 1) + d * tk
                k0 = pl.multiple_of(q0 + d * tk, tk)
                carry = step(carry, q, kb_ref[pl.ds(k0, tk), :], vb_ref[pl.ds(k0, tk), :],
                             col < row, suffix_matrix(tk))
            n_full = qi * (tq // tk)

        def full_block(i, carry):
            k0 = pl.multiple_of((n_full - 1 - i) * tk, tk)
            return step(carry, q, kb_ref[pl.ds(k0, tk), :], vb_ref[pl.ds(k0, tk), :],
                        None, suffix_matrix(tk))

        _, acc = lax.fori_loop(0, n_full, full_block, carry)
        o_ref[0, pl.ds(q0, tq), :] = acc.astype(BF16)
        return 0

    lax.fori_loop(0, sq // tq, q_tile, 0)


def _sb_attention(q, k, v, tail, tq, tk):
    b, sq, _ = q.shape
    sm = k.shape[1]
    has_tail = tail is not None
    head = lambda rows: pl.BlockSpec((1, rows, HEAD_DIM), lambda i, h: (i, 0, h))
    in_specs = [head(sq), head(sm), head(sm)]
    args = [q, k, v]
    if has_tail:
        in_specs += [head(sq), head(sq)]
        args += list(tail)
    return pl.pallas_call(
        functools.partial(_sb_kernel, tq=tq, tk=tk, has_tail=has_tail),
        grid=(b, N_HEADS),
        in_specs=in_specs,
        out_specs=head(sq),
        out_shape=jax.ShapeDtypeStruct((b, sq, D_MODEL), BF16),
        scratch_shapes=[pltpu.VMEM((sm, HEAD_DIM), BF16), pltpu.VMEM((sm, HEAD_DIM), BF16)],
        compiler_params=_cparams(2),
        name="sb_tail" if has_tail else "sb",
    )(*args)


def _swa_kernel(q_ref, k_ref, v_ref, sink_ref, o_ref, *, rq, rk, n_chunks, q_pos0, k_pos_off, k_base):
    t = pl.program_id(1)
    scale = C_HEAD_DIM ** -0.5
    lane = lax.broadcasted_iota(jnp.int32, (rq, LANES), 1)
    low = lane < C_HEAD_DIM
    rows = C_GROUP * rq

    def chunk(cc, _):
        c = t * n_chunks + cc
        kfirst = jnp.maximum(k_pos_off + c * rq, 0)
        krow = pl.multiple_of(kfirst - k_base, 8)
        r0 = pl.multiple_of(cc * rq, rq)
        qc = q_ref[0, pl.ds(r0, rq), :].astype(F32)
        kband = k_ref[0, pl.ds(krow, rk), :]
        vband = v_ref[0, pl.ds(krow, rk), :]
        qpos = q_pos0 + c * rq + lax.broadcasted_iota(jnp.int32, (rows, rk), 0) % rq
        kpos = kfirst + lax.broadcasted_iota(jnp.int32, (rows, rk), 1)
        dist = qpos // CHUNK - kpos // CHUNK
        vis = (dist >= 0) & (dist <= C_WINDOW_CHUNKS)
        for h in range(C_KV_HEADS):
            kh = kband[:, h * LANES:(h + 1) * LANES].astype(BF16)
            vh = vband[:, h * LANES:(h + 1) * LANES].astype(BF16)
            parts = []
            for g in range(C_GROUP):
                col = (h * C_GROUP + g) // 2 * LANES
                pair = qc[:, col:col + LANES]
                keep = low if g % 2 == 0 else jnp.logical_not(low)
                parts.append(jnp.where(keep, pair, 0.0).astype(BF16))
            qs = jnp.concatenate(parts, axis=0)
            s = _dot_nt(qs, kh) * scale
            s = jnp.where(vis, s, -jnp.inf)
            sink = sink_ref[h]
            m = jnp.maximum(jnp.max(s, axis=1, keepdims=True), sink)
            e = jnp.exp(s - m)
            p = e / (jnp.sum(e, axis=1, keepdims=True) + jnp.exp(sink - m))
            o = _dot(p.astype(BF16), vh)
            for pr in range(C_GROUP // 2):
                oa = o[(2 * pr) * rq:(2 * pr + 1) * rq]
                ob = o[(2 * pr + 1) * rq:(2 * pr + 2) * rq]
                col = (h * C_GROUP // 2 + pr) * LANES
                o_ref[0, pl.ds(r0, rq), col:col + LANES] = jnp.where(low, oa, ob).astype(BF16)
        return 0

    lax.fori_loop(0, n_chunks, chunk, 0)


def _swa_attention(q, k2, v2, sink_col, rq, rk, n_chunks, q_pos0, k_pos_off, k_base):
    b, sq, _ = q.shape
    sk = k2.shape[1]
    tq = rq * n_chunks
    kv = pl.BlockSpec((1, sk, C_KV_HEADS * LANES), lambda i, t: (i, 0, 0))
    qs = pl.BlockSpec((1, tq, D_MODEL), lambda i, t: (i, t, 0))
    return pl.pallas_call(
        functools.partial(_swa_kernel, rq=rq, rk=rk, n_chunks=n_chunks, q_pos0=q_pos0,
                          k_pos_off=k_pos_off, k_base=k_base),
        grid=(b, sq // tq),
        in_specs=[qs, kv, kv, pl.BlockSpec(sink_col.shape, lambda i, t: (0, 0, 0))],
        out_specs=qs,
        out_shape=jax.ShapeDtypeStruct((b, sq, D_MODEL), BF16),
        compiler_params=_cparams(2),
        name="swa",
    )(q, k2, v2, sink_col)


def _rope_tables(pos):
    half = ROPE_DIM // 2
    inv = ROPE_THETA ** (-jnp.arange(half, dtype=F32) / half)
    ang = pos.astype(F32)[:, None] * inv[None, :]
    cos, sin = jnp.cos(ang), jnp.sin(ang)
    n = pos.shape[0]
    ones = jnp.ones((n, C_HEAD_DIM - ROPE_DIM), F32)
    zeros = jnp.zeros((n, C_HEAD_DIM - ROPE_DIM), F32)
    zh = jnp.zeros((n, half), F32)
    cos64 = jnp.concatenate([cos, cos, ones], axis=1)
    sa64 = jnp.concatenate([-sin, zh, zeros], axis=1)
    sb64 = jnp.concatenate([zh, sin, zeros], axis=1)
    dup = lambda a: jnp.concatenate([a, a], axis=1)
    return dup(cos64), dup(sa64), dup(sb64)


def _dup_heads(a):
    lead = a.shape[:-1]
    a = a.reshape(lead + (C_KV_HEADS, 1, C_HEAD_DIM))
    a = jnp.broadcast_to(a, lead + (C_KV_HEADS, 2, C_HEAD_DIM))
    return a.reshape(lead + (C_KV_HEADS * LANES,))


def _undup_heads(a):
    lead = a.shape[:-1]
    return a.reshape(lead + (C_KV_HEADS, 2, C_HEAD_DIM))[..., 0, :]


def kernel(x_prompt, x_sample, cache_a_k, cache_a_v, cache_a_logf, cache_b_k, cache_b_v,
           cache_c_k, cache_c_v, norm_g, ffn_w_in, ffn_w_out, a_w_in, a_b_f, a_w_out,
           b_w_in, b_w_out, c_w_in, c_sink, c_w_out):
    depth = norm_g.shape[0]
    bp, sp, _ = x_prompt.shape
    bs, ss, _ = x_sample.shape
    p_len = cache_a_k.shape[2]
    c_buf = cache_c_k.shape[2]
    tp, ts = bp * sp, bs * ss
    tm_p, tm_s, tf = 512, ts, 512

    w_in_b = ffn_w_in.astype(BF16)
    w_out_b = ffn_w_out.astype(BF16)
    aw = D_MODEL
    a_w3 = jnp.stack([a_w_in[:, :, i * aw:(i + 1) * aw] for i in range(3)], axis=1).astype(BF16)
    a_wf = jnp.pad(a_w_in[:, :, 3 * aw:], ((0, 0), (0, 0), (0, LANES - N_HEADS))).astype(BF16)
    a_bf = jnp.pad(a_b_f, ((0, 0), (0, LANES - N_HEADS)))[:, None, :]
    b_w3 = jnp.stack([b_w_in[:, :, i * aw:(i + 1) * aw] for i in range(3)], axis=1).astype(BF16)
    kvw = C_KV_HEADS * C_HEAD_DIM
    c_wq = c_w_in[:, :, :D_MODEL].astype(BF16)
    c_wk2 = _dup_heads(c_w_in[:, :, D_MODEL:D_MODEL + kvw]).astype(BF16)
    c_wv2 = _dup_heads(c_w_in[:, :, D_MODEL + kvw:]).astype(BF16)
    a_w_out_b = a_w_out.astype(BF16)
    b_w_out_b = b_w_out.astype(BF16)
    c_w_out_b = c_w_out.astype(BF16)
    rope_p = _rope_tables(jnp.arange(sp))
    rope_s = _rope_tables(p_len + jnp.arange(ss))
    rope_s = tuple(jnp.tile(a, (bs, 1)) for a in rope_s)

    xp = x_prompt.reshape(tp, D_MODEL)
    xs = x_sample.reshape(ts, D_MODEL)
    outs = {n: [] for n in ("akp", "avp", "alp", "bkp", "bvp", "ckp", "cvp",
                            "aks", "avs", "als", "bks", "bvs", "cks", "cvs")}

    def heads(a, b, s):
        return a.reshape(b, s, N_HEADS, HEAD_DIM)

    for layer in range(depth):
        g = norm_g[layer]
        xp = _half_ffn(xp, g[0], g[1], w_in_b[layer, 0], w_out_b[layer, 0], tm_p, tf)
        xs = _half_ffn(xs, g[0], g[1], w_in_b[layer, 0], w_out_b[layer, 0], tm_s, tf)
        kind, j = layer % 3, layer // 3
        if kind == 0:
            q, k, v, lf = _proj_ab(xp, g[2], a_w3[j], a_wf[j], a_bf[j], tm_p)
            lft = jnp.transpose(lf.reshape(bp, sp, N_HEADS), (0, 2, 1))
            cum = _cumsum_lanes(lft.reshape(bp * N_HEADS, sp)).reshape(bp, N_HEADS, 1, sp)
            o = _fox_attention(q.reshape(bp, sp, D_MODEL), k.reshape(bp, sp, D_MODEL),
                               v.reshape(bp, sp, D_MODEL), cum, None, 512, 512)
            xp = _post(xp, o.reshape(tp, D_MODEL), a_w_out_b[j], g[3], tm_p)
            outs["akp"].append(heads(k, bp, sp)); outs["avp"].append(heads(v, bp, sp))
            outs["alp"].append(lf.reshape(bp, sp, N_HEADS))

            q, k, v, lf = _proj_ab(xs, g[2], a_w3[j], a_wf[j], a_bf[j], tm_s)
            lf_all = jnp.concatenate([cache_a_logf[j], lf.reshape(bs, ss, N_HEADS)], axis=1)
            n_all = p_len + ss
            n_pad = -n_all % LANES
            lft = jnp.pad(jnp.transpose(lf_all, (0, 2, 1)), ((0, 0), (0, 0), (0, n_pad)))
            cum = _cumsum_lanes(lft.reshape(bs * N_HEADS, n_all + n_pad))
            cum = cum.reshape(bs, N_HEADS, 1, n_all + n_pad)
            tail = (k.reshape(bs, ss, D_MODEL), v.reshape(bs, ss, D_MODEL), cum[..., p_len:n_all])
            o = _fox_attention(q.reshape(bs, ss, D_MODEL), cache_a_k[j].reshape(bs, p_len, D_MODEL),
                               cache_a_v[j].reshape(bs, p_len, D_MODEL), cum[..., :p_len], tail, ss, 512)
            xs = _post(xs, o.reshape(ts, D_MODEL), a_w_out_b[j], g[3], tm_s)
            outs["aks"].append(heads(k, bs, ss)); outs["avs"].append(heads(v, bs, ss))
            outs["als"].append(lf.reshape(bs, ss, N_HEADS))
        elif kind == 1:
            q, k, v = _proj_ab(xp, g[2], b_w3[j], None, None, tm_p)
            o = _sb_attention(q.reshape(bp, sp, D_MODEL), k.reshape(bp, sp, D_MODEL),
                              v.reshape(bp, sp, D_MODEL), None, 256, 128)
            xp = _post(xp, o.reshape(tp, D_MODEL), b_w_out_b[j], g[3], tm_p)
            outs["bkp"].append(heads(k, bp, sp)); outs["bvp"].append(heads(v, bp, sp))

            q, k, v = _proj_ab(xs, g[2], b_w3[j], None, None, tm_s)
            tail = (k.reshape(bs, ss, D_MODEL), v.reshape(bs, ss, D_MODEL))
            o = _sb_attention(q.reshape(bs, ss, D_MODEL), cache_b_k[j].reshape(bs, p_len, D_MODEL),
                              cache_b_v[j].reshape(bs, p_len, D_MODEL), tail, ss, 128)
            xs = _post(xs, o.reshape(ts, D_MODEL), b_w_out_b[j], g[3], tm_s)
            outs["bks"].append(heads(k, bs, ss)); outs["bvs"].append(heads(v, bs, ss))
        else:
            sink_col = lambda rq: jnp.repeat(
                c_sink[j].astype(F32).reshape(C_KV_HEADS, C_GROUP), rq, axis=1)[..., None]
            pad = C_WINDOW_CHUNKS * CHUNK
            q, k2, v2 = _proj_c(xp, g[2], c_wq[j], c_wk2[j], c_wv2[j], *rope_p, tm_p, sp)
            k2 = k2.reshape(bp, sp, -1)
            v2 = v2.reshape(bp, sp, -1)
            o = _swa_attention(q.reshape(bp, sp, D_MODEL), k2, v2, sink_col(CHUNK),
                               CHUNK, pad + CHUNK, 8, 0, -pad, 0)
            xp = _post(xp, o.reshape(tp, D_MODEL), c_w_out_b[j], g[3], tm_p)
            buf = min(C_WINDOW, sp)
            kv_shape = (bp, buf, C_KV_HEADS, C_HEAD_DIM)
            outs["ckp"].append(_undup_heads(k2[:, sp - buf:]).reshape(kv_shape))
            outs["cvp"].append(_undup_heads(v2[:, sp - buf:]).reshape(kv_shape))

            q, k2, v2 = _proj_c(xs, g[2], c_wq[j], c_wk2[j], c_wv2[j], *rope_s, tm_s, ts)
            kk2 = jnp.concatenate([_dup_heads(cache_c_k[j].reshape(bs, c_buf, kvw)),
                                   k2.reshape(bs, ss, -1)], axis=1)
            vv2 = jnp.concatenate([_dup_heads(cache_c_v[j].reshape(bs, c_buf, kvw)),
                                   v2.reshape(bs, ss, -1)], axis=1)
            o = _swa_attention(q.reshape(bs, ss, D_MODEL), kk2, vv2, sink_col(ss),
                               ss, c_buf + ss, 1, p_len, p_len - c_buf, p_len - c_buf)
            xs = _post(xs, o.reshape(ts, D_MODEL), c_w_out_b[j], g[3], tm_s)
            kv_shape = (bs, c_buf, C_KV_HEADS, C_HEAD_DIM)
            outs["cks"].append(_undup_heads(kk2[:, -c_buf:]).reshape(kv_shape))
            outs["cvs"].append(_undup_heads(vv2[:, -c_buf:]).reshape(kv_shape))
        xp = _half_ffn(xp, g[4], g[5], w_in_b[layer, 1], w_out_b[layer, 1], tm_p, tf)
        xs = _half_ffn(xs, g[4], g[5], w_in_b[layer, 1], w_out_b[layer, 1], tm_s, tf)

    st = lambda n: jnp.stack(outs[n])
    return (xp.reshape(bp, sp, D_MODEL), xs.reshape(bs, ss, D_MODEL),
            st("akp"), st("avp"), st("alp"), st("bkp"), st("bvp"), st("ckp"), st("cvp"),
            st("aks"), st("avs"), st("als"), st("bks"), st("bvs"), st("cks"), st("cvs"))
```

```python
import functools
import math

import jax
import jax.numpy as jnp
import numpy as np
from jax import lax
from jax.experimental import pallas as pl
from jax.experimental.pallas import tpu as pltpu

F32 = jnp.float32
BF16 = jnp.bfloat16

D_MODEL = 2048
D_FF = 11 * D_MODEL // 4
HEAD_DIM = 128
N_HEADS = 16
C_Q_HEADS = 32
C_KV_HEADS = 4
C_GROUP = C_Q_HEADS // C_KV_HEADS
C_HEAD_DIM = 64
C_WINDOW = 128
CHUNK = 64
C_WINDOW_CHUNKS = C_WINDOW // CHUNK
ROPE_THETA = 500000.0
ROPE_DIM = C_HEAD_DIM // 4
FFN_RES = 0.5
NORM_EPS = 1e-6
LOG2E = math.log2(math.e)
LANES = 128
VMEM_LIMIT = 56 * 1024 * 1024


def _cparams(n_grid):
    return pltpu.CompilerParams(dimension_semantics=("arbitrary",) * n_grid,
                                vmem_limit_bytes=VMEM_LIMIT)


def _rms(x, g):
    ms = jnp.mean(x * x, axis=-1, keepdims=True)
    return (x * lax.rsqrt(ms + NORM_EPS)) * g


def _dot(a, b):
    return jnp.dot(a, b, preferred_element_type=F32)


def _dot_nt(a, b):
    return lax.dot_general(a, b, (((1,), (1,)), ((), ())), preferred_element_type=F32)


def _ffn_kernel(x_ref, gpre_ref, gpost_ref, wg_ref, wu_ref, wo_ref, o_ref, xn_ref, acc_ref):
    j = pl.program_id(1)

    @pl.when(j == 0)
    def _():
        xn_ref[...] = _rms(x_ref[...], gpre_ref[...]).astype(BF16)
        acc_ref[...] = jnp.zeros_like(acc_ref)

    xn = xn_ref[...]
    gate = _dot(xn, wg_ref[...])
    up = _dot(xn, wu_ref[...])
    a = (gate * jax.nn.sigmoid(gate)) * up
    acc_ref[...] += _dot(a.astype(BF16), wo_ref[...])

    @pl.when(j == pl.num_programs(1) - 1)
    def _():
        o_ref[...] = x_ref[...] + FFN_RES * _rms(acc_ref[...], gpost_ref[...])


def _half_ffn(x, g_pre, g_post, w_in, w_out, layer, half, tm, tf):
    t = x.shape[0]
    nf = D_FF // tf
    return pl.pallas_call(
        _ffn_kernel,
        grid=(t // tm, nf),
        in_specs=[
            pl.BlockSpec((tm, D_MODEL), lambda i, j: (i, 0)),
            pl.BlockSpec((1, D_MODEL), lambda i, j: (0, 0)),
            pl.BlockSpec((1, D_MODEL), lambda i, j: (0, 0)),
            pl.BlockSpec((None, None, D_MODEL, tf), lambda i, j: (layer, half, 0, j)),
            pl.BlockSpec((None, None, D_MODEL, tf), lambda i, j: (layer, half, 0, j + nf)),
            pl.BlockSpec((None, None, tf, D_MODEL), lambda i, j: (layer, half, j, 0)),
        ],
        out_specs=pl.BlockSpec((tm, D_MODEL), lambda i, j: (i, 0)),
        out_shape=jax.ShapeDtypeStruct((t, D_MODEL), F32),
        scratch_shapes=[pltpu.VMEM((tm, D_MODEL), BF16), pltpu.VMEM((tm, D_MODEL), F32)],
        compiler_params=_cparams(2),
        name="half_ffn",
    )(x, g_pre.reshape(1, D_MODEL), g_post.reshape(1, D_MODEL), w_in, w_in, w_out)


def _log_sigmoid(x):
    return jnp.minimum(x, 0.0) - jnp.log1p(jnp.exp(-jnp.abs(x)))


def _proj_ab_kernel(*refs, with_forget):
    if with_forget:
        x_ref, g_ref, w_ref, wf_ref, bf_ref, q_ref, k_ref, v_ref, lf_ref, xn_ref = refs
    else:
        x_ref, g_ref, w_ref, q_ref, k_ref, v_ref, xn_ref = refs
    j = pl.program_id(1)

    @pl.when(j == 0)
    def _():
        xn = _rms(x_ref[...], g_ref[...]).astype(BF16)
        xn_ref[...] = xn
        if with_forget:
            f = _dot(xn, wf_ref[...]) + bf_ref[...]
            lf_ref[...] = _log_sigmoid(f)[:, :N_HEADS]

    y = _dot(xn_ref[...], w_ref[0])

    @pl.when(j == 0)
    def _():
        q_ref[...] = y.astype(BF16)

    @pl.when(j == 1)
    def _():
        k_ref[...] = y

    @pl.when(j == 2)
    def _():
        v_ref[...] = y


def _proj_ab(x, g, w3, wf, bf, tm):
    t = x.shape[0]
    with_forget = wf is not None
    row = pl.BlockSpec((tm, D_MODEL), lambda i, j: (i, 0))
    in_specs = [row, pl.BlockSpec((1, D_MODEL), lambda i, j: (0, 0)),
                pl.BlockSpec((1, D_MODEL, D_MODEL), lambda i, j: (j, 0, 0))]
    args = [x, g.reshape(1, D_MODEL), w3]
    out_specs = [row, row, row]
    out_shape = [jax.ShapeDtypeStruct((t, D_MODEL), BF16),
                 jax.ShapeDtypeStruct((t, D_MODEL), F32),
                 jax.ShapeDtypeStruct((t, D_MODEL), F32)]
    if with_forget:
        in_specs += [pl.BlockSpec((D_MODEL, LANES), lambda i, j: (0, 0)),
                     pl.BlockSpec((1, LANES), lambda i, j: (0, 0))]
        args += [wf, bf]
        out_specs.append(pl.BlockSpec((tm, N_HEADS), lambda i, j: (i, 0)))
        out_shape.append(jax.ShapeDtypeStruct((t, N_HEADS), F32))
    return pl.pallas_call(
        functools.partial(_proj_ab_kernel, with_forget=with_forget),
        grid=(t // tm, 3),
        in_specs=in_specs,
        out_specs=out_specs,
        out_shape=out_shape,
        scratch_shapes=[pltpu.VMEM((tm, D_MODEL), BF16)],
        compiler_params=_cparams(2),
        name="proj_a" if with_forget else "proj_b",
    )(*args)


def _rope_cols(y, cos, sa, sb):
    outs = []
    for c in range(y.shape[1] // LANES):
        blk = y[:, c * LANES:(c + 1) * LANES]
        nxt = pltpu.roll(blk, LANES - ROPE_DIM // 2, axis=1)
        prv = pltpu.roll(blk, ROPE_DIM // 2, axis=1)
        outs.append(blk * cos + nxt * sa + prv * sb)
    return outs


def _proj_c_kernel(x_ref, g_ref, wq_ref, wk_ref, wv_ref, cos_ref, sa_ref, sb_ref,
                   q_ref, k_ref, v_ref):
    xn = _rms(x_ref[...], g_ref[...]).astype(BF16)
    cos, sa, sb = cos_ref[...], sa_ref[...], sb_ref[...]
    q = _dot(xn, wq_ref[...])
    for c, blk in enumerate(_rope_cols(q, cos, sa, sb)):
        q_ref[:, c * LANES:(c + 1) * LANES] = blk.astype(BF16)
    k = _dot(xn, wk_ref[...])
    for c, blk in enumerate(_rope_cols(k, cos, sa, sb)):
        k_ref[:, c * LANES:(c + 1) * LANES] = blk
    v_ref[...] = _dot(xn, wv_ref[...])


def _proj_c(x, g, wq, wk2, wv2, cos, sa, sb, tm, rows_per_seq):
    t = x.shape[0]
    kvw = C_KV_HEADS * LANES
    nseq = rows_per_seq // tm
    row = lambda w: pl.BlockSpec((tm, w), lambda i: (i, 0))
    const = lambda a: pl.BlockSpec(a.shape, lambda i: (0, 0))
    tab = pl.BlockSpec((tm, LANES), lambda i: (i % nseq, 0))
    return pl.pallas_call(
        _proj_c_kernel,
        grid=(t // tm,),
        in_specs=[row(D_MODEL), pl.BlockSpec((1, D_MODEL), lambda i: (0, 0)),
                  const(wq), const(wk2), const(wv2), tab, tab, tab],
        out_specs=[row(D_MODEL), row(kvw), row(kvw)],
        out_shape=[jax.ShapeDtypeStruct((t, D_MODEL), BF16),
                   jax.ShapeDtypeStruct((t, kvw), F32),
                   jax.ShapeDtypeStruct((t, kvw), F32)],
        compiler_params=_cparams(1),
        name="proj_c",
    )(x, g.reshape(1, D_MODEL), wq, wk2, wv2, cos, sa, sb)


def _post_kernel(x_ref, o_ref, w_ref, g_ref, out_ref):
    y = _dot(o_ref[...], w_ref[...])
    out_ref[...] = x_ref[...] + _rms(y, g_ref[...])


def _post(x, o, w, g, tm):
    t = x.shape[0]
    row = pl.BlockSpec((tm, D_MODEL), lambda i: (i, 0))
    return pl.pallas_call(
        _post_kernel,
        grid=(t // tm,),
        in_specs=[row, row, pl.BlockSpec((D_MODEL, D_MODEL), lambda i: (0, 0)),
                  pl.BlockSpec((1, D_MODEL), lambda i: (0, 0))],
        out_specs=row,
        out_shape=jax.ShapeDtypeStruct((t, D_MODEL), F32),
        compiler_params=_cparams(1),
        name="mixer_out",
    )(x, o, w, g.reshape(1, D_MODEL))


def _cumsum_kernel(x_ref, o_ref):
    rows, length = x_ref.shape
    r = lax.broadcasted_iota(jnp.int32, (LANES, LANES), 0)
    c = lax.broadcasted_iota(jnp.int32, (LANES, LANES), 1)
    upper = (r <= c).astype(F32).astype(BF16)
    carry = jnp.zeros((rows, 1), F32)
    for b in range(length // LANES):
        x = x_ref[:, b * LANES:(b + 1) * LANES]
        hi = x.astype(BF16)
        r1 = x - hi.astype(F32)
        mid = r1.astype(BF16)
        lo = (r1 - mid.astype(F32)).astype(BF16)
        cs = _dot(hi, upper) + _dot(mid, upper) + _dot(lo, upper)
        o_ref[:, b * LANES:(b + 1) * LANES] = cs + carry
        carry = carry + cs[:, LANES - 1:LANES]


def _cumsum_lanes(x):
    return pl.pallas_call(
        _cumsum_kernel,
        out_shape=jax.ShapeDtypeStruct(x.shape, F32),
        compiler_params=pltpu.CompilerParams(vmem_limit_bytes=VMEM_LIMIT),
        name="cumsum",
    )(x)


FOX_HEADS = 2


def _fox_kernel(*refs, tq, tk, has_tail):
    if has_tail:
        q_ref, k_ref, v_ref, c_ref, kt_ref, vt_ref, ct_ref, o_ref, kb_ref, vb_ref, c2_ref = refs
    else:
        q_ref, k_ref, v_ref, c_ref, o_ref, kb_ref, vb_ref, c2_ref = refs
    sq = q_ref.shape[1]
    sm = k_ref.shape[1]
    hd = HEAD_DIM
    scale2 = hd ** -0.5 * LOG2E
    heads = range(FOX_HEADS)
    cols = [slice(h * hd, (h + 1) * hd) for h in heads]
    kb_ref[...] = k_ref[0].astype(BF16)
    for h in heads:
        vb_ref[h, :, :hd] = v_ref[0, :, cols[h]].astype(BF16)
        vb_ref[h, :, hd:] = jnp.ones((sm, hd), BF16)
        c2_ref[h] = c_ref[0, h] * LOG2E

    def step(carry, qs, ks, v1s, crows, mask):
        scores = [_dot_nt(q, k) * scale2 - crow for q, k, crow in zip(qs, ks, crows)]
        out = []
        for (m, acc), s, v1 in zip(carry, scores, v1s):
            if mask is not None:
                s = jnp.where(mask, s, -jnp.inf)
            m_new = jnp.maximum(m, jnp.max(s, axis=1, keepdims=True))
            p = jnp.exp2(s - m_new)
            out.append((m_new, jnp.exp2(m - m_new) * acc + _dot(p.astype(BF16), v1)))
        return tuple(out)

    def main_step(carry, qs, k0, width, mask):
        rows = pl.ds(k0, width)
        return step(carry, qs, [kb_ref[rows, cols[h]] for h in heads],
                    [vb_ref[h, rows, :] for h in heads], [c2_ref[h, :, rows] for h in heads], mask)

    def q_tile(qi, _):
        q0 = pl.multiple_of(qi * tq, tq)
        qs = [q_ref[0, pl.ds(q0, tq), cols[h]] for h in heads]
        carry = tuple((jnp.full((tq, 1), -jnp.inf, F32), jnp.zeros((tq, 2 * hd), F32))
                      for _ in heads)
        n_full = jnp.int32(sm // tk) if has_tail else qi
        carry = lax.fori_loop(
            0, n_full // 2,
            lambda j, cr: main_step(cr, qs, pl.multiple_of(j * 2 * tk, 2 * tk), 2 * tk, None), carry)
        carry = lax.fori_loop(
            0, n_full % 2,
            lambda _, cr: main_step(cr, qs, pl.multiple_of((n_full - 1) * tk, tk), tk, None), carry)
        if has_tail:
            nt = kt_ref.shape[1]
            row = lax.broadcasted_iota(jnp.int32, (tq, nt), 0)
            col = lax.broadcasted_iota(jnp.int32, (tq, nt), 1)
            ones = jnp.ones((nt, hd), BF16)
            carry = step(carry, qs, [kt_ref[0, :, cols[h]].astype(BF16) for h in heads],
                         [jnp.concatenate([vt_ref[0, :, cols[h]].astype(BF16), ones], axis=1)
                          for h in heads],
                         [ct_ref[0, h] * LOG2E for h in heads], col <= row)
        else:
            row = lax.broadcasted_iota(jnp.int32, (tq, tk), 0)
            col = lax.broadcasted_iota(jnp.int32, (tq, tk), 1)
            carry = main_step(carry, qs, q0, tk, col <= row)
        for h, (_, acc) in zip(heads, carry):
            o_ref[0, pl.ds(q0, tq), cols[h]] = (acc[:, :hd] / acc[:, hd:]).astype(BF16)
        return 0

    lax.fori_loop(0, sq // tq, q_tile, 0)


def _fox_attention(q, k, v, c, tail, tq, tk):
    b, sq, _ = q.shape
    sm = k.shape[1]
    has_tail = tail is not None
    width = FOX_HEADS * HEAD_DIM
    head = lambda rows: pl.BlockSpec((1, rows, width), lambda i, h: (i, 0, h))
    crow = lambda n: pl.BlockSpec((1, FOX_HEADS, 1, n), lambda i, h: (i, h, 0, 0))
    in_specs = [head(sq), head(sm), head(sm), crow(sm)]
    args = [q, k, v, c]
    if has_tail:
        in_specs += [head(sq), head(sq), crow(sq)]
        args += list(tail)
    return pl.pallas_call(
        functools.partial(_fox_kernel, tq=tq, tk=tk, has_tail=has_tail),
        grid=(b, N_HEADS // FOX_HEADS),
        in_specs=in_specs,
        out_specs=head(sq),
        out_shape=jax.ShapeDtypeStruct((b, sq, D_MODEL), BF16),
        scratch_shapes=[pltpu.VMEM((sm, width), BF16),
                        pltpu.VMEM((FOX_HEADS, sm, 2 * HEAD_DIM), BF16),
                        pltpu.VMEM((FOX_HEADS, 1, sm), F32)],
        compiler_params=_cparams(2),
        name="fox_tail" if has_tail else "fox",
    )(*args)


SB_BLOCK = 128
SB_WINDOW = 3
SB_GROUP = 4
SB_UNDERFLOW = -104.0


def _softplus(z):
    return jnp.maximum(z, 0.0) + jnp.log(1.0 + jnp.exp(-jnp.abs(z)))


def _sb_block(q, k, v, run, before, suffix, packed):
    z = _dot_nt(q, k) * (HEAD_DIM ** -0.5)
    sp = _softplus(z)
    log_fail = -sp
    if before is not None:
        log_fail = jnp.where(before, log_fail, 0.0)
    hi = log_fail.astype(BF16)
    lo = (log_fail - hi.astype(F32)).astype(BF16)
    if packed:
        later = _dot(jnp.concatenate([hi, lo], axis=1), suffix) + run
    else:
        later = _dot(hi, suffix) + _dot(lo, suffix) + run
    w = jnp.exp((z - sp) + later)
    if before is not None:
        w = jnp.where(before, w, 0.0)
    return _dot(w.astype(BF16), v), run + jnp.sum(log_fail, axis=1, keepdims=True)


def _sb_kernel(*refs, has_tail):
    if has_tail:
        q_ref, k_ref, v_ref, kt_ref, vt_ref, o_ref, kb_ref, vb_ref = refs
    else:
        q_ref, k_ref, v_ref, o_ref, kb_ref, vb_ref = refs
    blk = SB_BLOCK
    sq = q_ref.shape[1]
    sm = k_ref.shape[1]
    kb_ref[...] = k_ref[0].astype(BF16)
    vb_ref[...] = v_ref[0].astype(BF16)
    r = lax.broadcasted_iota(jnp.int32, (blk, blk), 0)
    c = lax.broadcasted_iota(jnp.int32, (blk, blk), 1)
    suffix = (r > c).astype(F32).astype(BF16)
    suffix2 = jnp.concatenate([suffix, suffix], axis=0)

    def main_block(q, j, run, before):
        k0 = pl.multiple_of(j * blk, blk)
        return _sb_block(q, kb_ref[pl.ds(k0, blk), :], vb_ref[pl.ds(k0, blk), :], run, before,
                         suffix2, True)

    def finish(q, j, run, acc, rows):
        def cond(carry):
            j, run, _ = carry
            return jnp.logical_and(j >= 0, jnp.max(run) > SB_UNDERFLOW)

        def body(carry):
            j, run, acc = carry
            pv, run = main_block(q, j, run, None)
            return j - 1, run, acc + pv

        _, _, acc = lax.while_loop(cond, body, (j, run, acc))
        o_ref[0, rows, :] = acc.astype(BF16)

    if has_tail:
        q = q_ref[0]
        nt = kt_ref.shape[1]
        row = lax.broadcasted_iota(jnp.int32, (sq, nt), 0)
        col = lax.broadcasted_iota(jnp.int32, (sq, nt), 1)
        tr = lax.broadcasted_iota(jnp.int32, (nt, nt), 0)
        tc = lax.broadcasted_iota(jnp.int32, (nt, nt), 1)
        acc, run = _sb_block(q, kt_ref[0].astype(BF16), vt_ref[0].astype(BF16),
                             jnp.zeros((sq, 1), F32), col < row,
                             (tr > tc).astype(F32).astype(BF16), False)
        finish(q, jnp.int32(sm // blk - 1), run, acc, slice(None))
        return

    def windows(tiles, n_older):
        nb = n_older + 1
        first = [pl.multiple_of((qi - n_older) * blk, blk) for _, qi in tiles]
        zs = []
        for (q, _), k0 in zip(tiles, first):
            z = _dot_nt(q, kb_ref[pl.ds(k0, nb * blk), :]) * (HEAD_DIM ** -0.5)
            zs.append([z[:, d * blk:(d + 1) * blk] for d in range(nb)])
        pre = []
        for blocks in zs:
            items = []
            for d, z in enumerate(blocks):
                sp = _softplus(z)
                log_fail = jnp.where(c < r, -sp, 0.0) if d == nb - 1 else -sp
                hi = log_fail.astype(BF16)
                lo = (log_fail - hi.astype(F32)).astype(BF16)
                items.append((z - sp, jnp.concatenate([hi, lo], axis=1),
                              jnp.sum(log_fail, axis=1, keepdims=True)))
            pre.append(items)
        laters = [[_dot(packed, suffix2) for _, packed, _ in items] for items in pre]
        outs = []
        for items, later, k0 in zip(pre, laters, first):
            run = jnp.zeros((blk, 1), F32)
            ws = [None] * nb
            for d in reversed(range(nb)):
                log_sig, _, row_sum = items[d]
                w = jnp.exp(log_sig + (later[d] + run))
                if d == nb - 1:
                    w = jnp.where(c < r, w, 0.0)
                ws[d] = w.astype(BF16)
                run = run + row_sum
            acc = _dot(jnp.concatenate(ws, axis=1), vb_ref[pl.ds(k0, nb * blk), :])
            outs += [run, acc]
        return tuple(outs)

    def group(p, _):
        qis = [SB_GROUP * p + i for i in range(SB_GROUP)]
        rows = [pl.ds(pl.multiple_of(qi * blk, blk), blk) for qi in qis]
        tiles = [(q_ref[0, rw, :], qi) for rw, qi in zip(rows, qis)]
        wide = qis[0] >= SB_WINDOW - 1
        res = lax.cond(wide, lambda: windows(tiles, SB_WINDOW - 1), lambda: windows(tiles, 0))
        done = jnp.where(wide, SB_WINDOW - 1, 0)
        for i, ((q, qi), rw) in enumerate(zip(tiles, rows)):
            finish(q, qi - 1 - done, res[2 * i], res[2 * i + 1], rw)
        return 0

    lax.fori_loop(0, sq // (SB_GROUP * blk), group, 0)


def _sb_attention(q, k, v, tail):
    b, sq, _ = q.shape
    sm = k.shape[1]
    has_tail = tail is not None
    assert sm % SB_BLOCK == 0 and (has_tail or (sq == sm and sq % (SB_GROUP * SB_BLOCK) == 0))
    head = lambda rows: pl.BlockSpec((1, rows, HEAD_DIM), lambda i, h: (i, 0, h))
    in_specs = [head(sq), head(sm), head(sm)]
    args = [q, k, v]
    if has_tail:
        in_specs += [head(sq), head(sq)]
        args += list(tail)
    return pl.pallas_call(
        functools.partial(_sb_kernel, has_tail=has_tail),
        grid=(b, N_HEADS),
        in_specs=in_specs,
        out_specs=head(sq),
        out_shape=jax.ShapeDtypeStruct((b, sq, D_MODEL), BF16),
        scratch_shapes=[pltpu.VMEM((sm, HEAD_DIM), BF16), pltpu.VMEM((sm, HEAD_DIM), BF16)],
        compiler_params=_cparams(2),
        name="sb_tail" if has_tail else "sb",
    )(*args)


def _swa_kernel(q_ref, k_ref, v_ref, sink_ref, o_ref, *, rq, rk, n_chunks, q_pos0, k_pos_off, k_base):
    t = pl.program_id(1)
    scale = C_HEAD_DIM ** -0.5
    lane = lax.broadcasted_iota(jnp.int32, (rq, LANES), 1)
    low = lane < C_HEAD_DIM
    rows = C_GROUP * rq

    row_pos = lax.broadcasted_iota(jnp.int32, (rows, 1), 0) % rq
    col_pos = lax.broadcasted_iota(jnp.int32, (1, rk), 1)

    def chunk(cc, _):
        c = t * n_chunks + cc
        kfirst = jnp.maximum(k_pos_off + c * rq, 0)
        krow = pl.multiple_of(kfirst - k_base, 8)
        r0 = pl.multiple_of(cc * rq, rq)
        qc = q_ref[0, pl.ds(r0, rq), :].astype(F32)
        kband = k_ref[0, pl.ds(krow, rk), :]
        vband = v_ref[0, pl.ds(krow, rk), :]
        dist = (q_pos0 + c * rq + row_pos) // CHUNK - (kfirst + col_pos) // CHUNK
        vis = (dist >= 0) & (dist <= C_WINDOW_CHUNKS)
        scores = []
        for h in range(C_KV_HEADS):
            kh = kband[:, h * LANES:(h + 1) * LANES].astype(BF16)
            parts = []
            for g in range(C_GROUP):
                col = (h * C_GROUP + g) // 2 * LANES
                pair = qc[:, col:col + LANES]
                keep = low if g % 2 == 0 else jnp.logical_not(low)
                parts.append(jnp.where(keep, pair, 0.0).astype(BF16))
            qs = jnp.concatenate(parts, axis=0)
            scores.append(jnp.where(vis, _dot_nt(qs, kh) * scale, -jnp.inf))
        probs = []
        for h, s in enumerate(scores):
            sink = sink_ref[h]
            m = jnp.maximum(jnp.max(s, axis=1, keepdims=True), sink)
            e = jnp.exp(s - m)
            inv = 1.0 / (jnp.sum(e, axis=1, keepdims=True) + jnp.exp(sink - m))
            probs.append((e.astype(BF16), inv))
        for h, (e, inv) in enumerate(probs):
            vh = vband[:, h * LANES:(h + 1) * LANES].astype(BF16)
            o = _dot(e, vh) * inv
            for pr in range(C_GROUP // 2):
                oa = o[(2 * pr) * rq:(2 * pr + 1) * rq]
                ob = o[(2 * pr + 1) * rq:(2 * pr + 2) * rq]
                col = (h * C_GROUP // 2 + pr) * LANES
                o_ref[0, pl.ds(r0, rq), col:col + LANES] = jnp.where(low, oa, ob).astype(BF16)
        return 0

    lax.fori_loop(0, n_chunks, chunk, 0)


def _swa_attention(q, k2, v2, sink_col, rq, rk, n_chunks, q_pos0, k_pos_off, k_base):
    b, sq, _ = q.shape
    sk = k2.shape[1]
    tq = rq * n_chunks
    kv = pl.BlockSpec((1, sk, C_KV_HEADS * LANES), lambda i, t: (i, 0, 0))
    qs = pl.BlockSpec((1, tq, D_MODEL), lambda i, t: (i, t, 0))
    return pl.pallas_call(
        functools.partial(_swa_kernel, rq=rq, rk=rk, n_chunks=n_chunks, q_pos0=q_pos0,
                          k_pos_off=k_pos_off, k_base=k_base),
        grid=(b, sq // tq),
        in_specs=[qs, kv, kv, pl.BlockSpec(sink_col.shape, lambda i, t: (0, 0, 0))],
        out_specs=qs,
        out_shape=jax.ShapeDtypeStruct((b, sq, D_MODEL), BF16),
        compiler_params=_cparams(2),
        name="swa",
    )(q, k2, v2, sink_col)


def _rope_tables(pos):
    half = ROPE_DIM // 2
    inv = ROPE_THETA ** (-jnp.arange(half, dtype=F32) / half)
    ang = pos.astype(F32)[:, None] * inv[None, :]
    cos, sin = jnp.cos(ang), jnp.sin(ang)
    n = pos.shape[0]
    ones = jnp.ones((n, C_HEAD_DIM - ROPE_DIM), F32)
    zeros = jnp.zeros((n, C_HEAD_DIM - ROPE_DIM), F32)
    zh = jnp.zeros((n, half), F32)
    cos64 = jnp.concatenate([cos, cos, ones], axis=1)
    sa64 = jnp.concatenate([-sin, zh, zeros], axis=1)
    sb64 = jnp.concatenate([zh, sin, zeros], axis=1)
    dup = lambda a: jnp.concatenate([a, a], axis=1)
    return dup(cos64), dup(sa64), dup(sb64)


def _dup_heads(a):
    lead = a.shape[:-1]
    a = a.reshape(lead + (C_KV_HEADS, 1, C_HEAD_DIM))
    a = jnp.broadcast_to(a, lead + (C_KV_HEADS, 2, C_HEAD_DIM))
    return a.reshape(lead + (C_KV_HEADS * LANES,))


def _undup_heads(a):
    lead = a.shape[:-1]
    return a.reshape(lead + (C_KV_HEADS, 2, C_HEAD_DIM))[..., 0, :]


def kernel(x_prompt, x_sample, cache_a_k, cache_a_v, cache_a_logf, cache_b_k, cache_b_v,
           cache_c_k, cache_c_v, norm_g, ffn_w_in, ffn_w_out, a_w_in, a_b_f, a_w_out,
           b_w_in, b_w_out, c_w_in, c_sink, c_w_out):
    depth = norm_g.shape[0]
    bp, sp, _ = x_prompt.shape
    bs, ss, _ = x_sample.shape
    p_len = cache_a_k.shape[2]
    c_buf = cache_c_k.shape[2]
    tp, ts = bp * sp, bs * ss
    tm_p, tm_s, tf = 512, ts, 512

    w_in_b = ffn_w_in.astype(BF16)
    w_out_b = ffn_w_out.astype(BF16)
    aw = D_MODEL
    a_w3 = jnp.stack([a_w_in[:, :, i * aw:(i + 1) * aw] for i in range(3)], axis=1).astype(BF16)
    a_wf = jnp.pad(a_w_in[:, :, 3 * aw:], ((0, 0), (0, 0), (0, LANES - N_HEADS))).astype(BF16)
    a_bf = jnp.pad(a_b_f, ((0, 0), (0, LANES - N_HEADS)))[:, None, :]
    b_w3 = jnp.stack([b_w_in[:, :, i * aw:(i + 1) * aw] for i in range(3)], axis=1).astype(BF16)
    kvw = C_KV_HEADS * C_HEAD_DIM
    c_wq = c_w_in[:, :, :D_MODEL].astype(BF16)
    c_wk2 = _dup_heads(c_w_in[:, :, D_MODEL:D_MODEL + kvw]).astype(BF16)
    c_wv2 = _dup_heads(c_w_in[:, :, D_MODEL + kvw:]).astype(BF16)
    a_w_out_b = a_w_out.astype(BF16)
    b_w_out_b = b_w_out.astype(BF16)
    c_w_out_b = c_w_out.astype(BF16)
    rope_p = _rope_tables(jnp.arange(sp))
    rope_s = _rope_tables(p_len + jnp.arange(ss))
    rope_s = tuple(jnp.tile(a, (bs, 1)) for a in rope_s)

    xp = x_prompt.reshape(tp, D_MODEL)
    xs = x_sample.reshape(ts, D_MODEL)
    outs = {n: [] for n in ("akp", "avp", "alp", "bkp", "bvp", "ckp", "cvp",
                            "aks", "avs", "als", "bks", "bvs", "cks", "cvs")}

    def heads(a, b, s):
        return a.reshape(b, s, N_HEADS, HEAD_DIM)

    for layer in range(depth):
        g = norm_g[layer]
        xp = _half_ffn(xp, g[0], g[1], w_in_b, w_out_b, layer, 0, tm_p, tf)
        xs = _half_ffn(xs, g[0], g[1], w_in_b, w_out_b, layer, 0, tm_s, tf)
        kind, j = layer % 3, layer // 3
        if kind == 0:
            q, k, v, lf = _proj_ab(xp, g[2], a_w3[j], a_wf[j], a_bf[j], tm_p)
            lft = jnp.transpose(lf.reshape(bp, sp, N_HEADS), (0, 2, 1))
            cum = _cumsum_lanes(lft.reshape(bp * N_HEADS, sp)).reshape(bp, N_HEADS, 1, sp)
            o = _fox_attention(q.reshape(bp, sp, D_MODEL), k.reshape(bp, sp, D_MODEL),
                               v.reshape(bp, sp, D_MODEL), cum, None, 512, 512)
            xp = _post(xp, o.reshape(tp, D_MODEL), a_w_out_b[j], g[3], tm_p)
            outs["akp"].append(heads(k, bp, sp)); outs["avp"].append(heads(v, bp, sp))
            outs["alp"].append(lf.reshape(bp, sp, N_HEADS))

            q, k, v, lf = _proj_ab(xs, g[2], a_w3[j], a_wf[j], a_bf[j], tm_s)
            lf_all = jnp.concatenate([cache_a_logf[j], lf.reshape(bs, ss, N_HEADS)], axis=1)
            n_all = p_len + ss
            n_pad = -n_all % LANES
            lft = jnp.pad(jnp.transpose(lf_all, (0, 2, 1)), ((0, 0), (0, 0), (0, n_pad)))
            cum = _cumsum_lanes(lft.reshape(bs * N_HEADS, n_all + n_pad))
            cum = cum.reshape(bs, N_HEADS, 1, n_all + n_pad)
            tail = (k.reshape(bs, ss, D_MODEL), v.reshape(bs, ss, D_MODEL), cum[..., p_len:n_all])
            o = _fox_attention(q.reshape(bs, ss, D_MODEL), cache_a_k[j].reshape(bs, p_len, D_MODEL),
                               cache_a_v[j].reshape(bs, p_len, D_MODEL), cum[..., :p_len], tail, ss, 512)
            xs = _post(xs, o.reshape(ts, D_MODEL), a_w_out_b[j], g[3], tm_s)
            outs["aks"].append(heads(k, bs, ss)); outs["avs"].append(heads(v, bs, ss))
            outs["als"].append(lf.reshape(bs, ss, N_HEADS))
        elif kind == 1:
            q, k, v = _proj_ab(xp, g[2], b_w3[j], None, None, tm_p)
            o = _sb_attention(q.reshape(bp, sp, D_MODEL), k.reshape(bp, sp, D_MODEL),
                              v.reshape(bp, sp, D_MODEL), None)
            xp = _post(xp, o.reshape(tp, D_MODEL), b_w_out_b[j], g[3], tm_p)
            outs["bkp"].append(heads(k, bp, sp)); outs["bvp"].append(heads(v, bp, sp))

            q, k, v = _proj_ab(xs, g[2], b_w3[j], None, None, tm_s)
            tail = (k.reshape(bs, ss, D_MODEL), v.reshape(bs, ss, D_MODEL))
            o = _sb_attention(q.reshape(bs, ss, D_MODEL), cache_b_k[j].reshape(bs, p_len, D_MODEL),
                              cache_b_v[j].reshape(bs, p_len, D_MODEL), tail)
            xs = _post(xs, o.reshape(ts, D_MODEL), b_w_out_b[j], g[3], tm_s)
            outs["bks"].append(heads(k, bs, ss)); outs["bvs"].append(heads(v, bs, ss))
        else:
            sink_col = lambda rq: jnp.repeat(
                c_sink[j].astype(F32).reshape(C_KV_HEADS, C_GROUP), rq, axis=1)[..., None]
            pad = C_WINDOW_CHUNKS * CHUNK
            q, k2, v2 = _proj_c(xp, g[2], c_wq[j], c_wk2[j], c_wv2[j], *rope_p, tm_p, sp)
            k2 = k2.reshape(bp, sp, -1)
            v2 = v2.reshape(bp, sp, -1)
            rq = 2 * CHUNK
            o = _swa_attention(q.reshape(bp, sp, D_MODEL), k2, v2, sink_col(rq),
                               rq, pad + rq, 4, 0, -pad, 0)
            xp = _post(xp, o.reshape(tp, D_MODEL), c_w_out_b[j], g[3], tm_p)
            buf = min(C_WINDOW, sp)
            kv_shape = (bp, buf, C_KV_HEADS, C_HEAD_DIM)
            outs["ckp"].append(_undup_heads(k2[:, sp - buf:]).reshape(kv_shape))
            outs["cvp"].append(_undup_heads(v2[:, sp - buf:]).reshape(kv_shape))

            q, k2, v2 = _proj_c(xs, g[2], c_wq[j], c_wk2[j], c_wv2[j], *rope_s, tm_s, ts)
            kk2 = jnp.concatenate([_dup_heads(cache_c_k[j].reshape(bs, c_buf, kvw)),
                                   k2.reshape(bs, ss, -1)], axis=1)
            vv2 = jnp.concatenate([_dup_heads(cache_c_v[j].reshape(bs, c_buf, kvw)),
                                   v2.reshape(bs, ss, -1)], axis=1)
            o = _swa_attention(q.reshape(bs, ss, D_MODEL), kk2, vv2, sink_col(ss),
                               ss, c_buf + ss, 1, p_len, p_len - c_buf, p_len - c_buf)
            xs = _post(xs, o.reshape(ts, D_MODEL), c_w_out_b[j], g[3], tm_s)
            kv_shape = (bs, c_buf, C_KV_HEADS, C_HEAD_DIM)
            outs["cks"].append(_undup_heads(kk2[:, -c_buf:]).reshape(kv_shape))
            outs["cvs"].append(_undup_heads(vv2[:, -c_buf:]).reshape(kv_shape))
        xp = _half_ffn(xp, g[4], g[5], w_in_b, w_out_b, layer, 1, tm_p, tf)
        xs = _half_ffn(xs, g[4], g[5], w_in_b, w_out_b, layer, 1, tm_s, tf)

    st = lambda n: jnp.stack(outs[n])
    return (xp.reshape(bp, sp, D_MODEL), xs.reshape(bs, ss, D_MODEL),
            st("akp"), st("avp"), st("alp"), st("bkp"), st("bvp"), st("ckp"), st("cvp"),
            st("aks"), st("avs"), st("als"), st("bks"), st("bvs"), st("cks"), st("cvs"))
```

```python
import functools
import math

import jax
import jax.numpy as jnp
import numpy as np
from jax import lax
from jax.experimental import pallas as pl
from jax.experimental.pallas import tpu as pltpu

F32 = jnp.float32
BF16 = jnp.bfloat16

D_MODEL = 2048
D_FF = 11 * D_MODEL // 4
HEAD_DIM = 128
N_HEADS = 16
C_Q_HEADS = 32
C_KV_HEADS = 4
C_GROUP = C_Q_HEADS // C_KV_HEADS
C_HEAD_DIM = 64
C_WINDOW = 128
CHUNK = 64
C_WINDOW_CHUNKS = C_WINDOW // CHUNK
ROPE_THETA = 500000.0
ROPE_DIM = C_HEAD_DIM // 4
FFN_RES = 0.5
NORM_EPS = 1e-6
LOG2E = math.log2(math.e)
LANES = 128
VMEM_LIMIT = 56 * 1024 * 1024


def _cparams(n_grid):
    return pltpu.CompilerParams(dimension_semantics=("arbitrary",) * n_grid,
                                vmem_limit_bytes=VMEM_LIMIT)


def _rms(x, g):
    ms = jnp.mean(x * x, axis=-1, keepdims=True)
    return (x * lax.rsqrt(ms + NORM_EPS)) * g


def _dot(a, b):
    return jnp.dot(a, b, preferred_element_type=F32)


def _dot_nt(a, b):
    return lax.dot_general(a, b, (((1,), (1,)), ((), ())), preferred_element_type=F32)


def _ffn_kernel(x_ref, gpre_ref, gpost_ref, wg_ref, wu_ref, wo_ref, o_ref, xn_ref, acc_ref):
    j = pl.program_id(1)

    @pl.when(j == 0)
    def _():
        xn_ref[...] = _rms(x_ref[...], gpre_ref[...]).astype(BF16)
        acc_ref[...] = jnp.zeros_like(acc_ref)

    xn = xn_ref[...]
    gate = _dot(xn, wg_ref[...])
    up = _dot(xn, wu_ref[...])
    a = (gate * jax.nn.sigmoid(gate)) * up
    acc_ref[...] += _dot(a.astype(BF16), wo_ref[...])

    @pl.when(j == pl.num_programs(1) - 1)
    def _():
        o_ref[...] = x_ref[...] + FFN_RES * _rms(acc_ref[...], gpost_ref[...])


def _half_ffn(x, g_pre, g_post, w_in, w_out, layer, half, tm, tf):
    t = x.shape[0]
    nf = D_FF // tf
    return pl.pallas_call(
        _ffn_kernel,
        grid=(t // tm, nf),
        in_specs=[
            pl.BlockSpec((tm, D_MODEL), lambda i, j: (i, 0)),
            pl.BlockSpec((1, D_MODEL), lambda i, j: (0, 0)),
            pl.BlockSpec((1, D_MODEL), lambda i, j: (0, 0)),
            pl.BlockSpec((None, None, D_MODEL, tf), lambda i, j: (layer, half, 0, j)),
            pl.BlockSpec((None, None, D_MODEL, tf), lambda i, j: (layer, half, 0, j + nf)),
            pl.BlockSpec((None, None, tf, D_MODEL), lambda i, j: (layer, half, j, 0)),
        ],
        out_specs=pl.BlockSpec((tm, D_MODEL), lambda i, j: (i, 0)),
        out_shape=jax.ShapeDtypeStruct((t, D_MODEL), F32),
        scratch_shapes=[pltpu.VMEM((tm, D_MODEL), BF16), pltpu.VMEM((tm, D_MODEL), F32)],
        compiler_params=_cparams(2),
        name="half_ffn",
    )(x, g_pre.reshape(1, D_MODEL), g_post.reshape(1, D_MODEL), w_in, w_in, w_out)


def _log_sigmoid(x):
    return jnp.minimum(x, 0.0) - jnp.log1p(jnp.exp(-jnp.abs(x)))


def _proj_ab_kernel(*refs, with_forget, n_prev):
    refs = list(refs)
    x_ref, g_ref, w_ref = refs[:3]
    del refs[:3]
    if with_forget:
        wf_ref, bf_ref = refs[:2]
        del refs[:2]
    if n_prev:
        kprev_ref, vprev_ref = refs[:2]
        del refs[:2]
    q_ref, kb_ref, vb_ref = refs[:3]
    del refs[:3]
    if with_forget:
        lf_ref = refs.pop(0)
    k4_ref, v4_ref, xn_ref, kbuf, vbuf, sem, carry_sem = refs
    i, j = pl.program_id(0), pl.program_id(1)
    tm = x_ref.shape[0]

    def head_copies(buf, dst, which, tile):
        rows = pl.ds(pl.multiple_of(tile * tm, tm), tm)
        return [pltpu.make_async_copy(buf.at[:, pl.ds(h * HEAD_DIM, HEAD_DIM)],
                                      dst.at[n_prev, rows, h, :], sem.at[which, h])
                for h in range(N_HEADS)]

    def carry_copies():
        return [pltpu.make_async_copy(kprev_ref, k4_ref.at[pl.ds(0, n_prev)], carry_sem.at[0]),
                pltpu.make_async_copy(vprev_ref, v4_ref.at[pl.ds(0, n_prev)], carry_sem.at[1])]

    @pl.when(j == 0)
    def _():
        xn = _rms(x_ref[...], g_ref[...]).astype(BF16)
        xn_ref[...] = xn
        if with_forget:
            f = _dot(xn, wf_ref[...]) + bf_ref[...]
            lf_ref[...] = _log_sigmoid(f)[:, :N_HEADS]
        if n_prev:
            @pl.when(i == 0)
            def _():
                for cp in carry_copies():
                    cp.start()

    y = _dot(xn_ref[...], w_ref[0])

    @pl.when(j == 0)
    def _():
        q_ref[...] = y.astype(BF16)

        @pl.when(i > 0)
        def _():
            for cp in head_copies(vbuf, v4_ref, 1, i - 1):
                cp.wait()

    @pl.when(j == 1)
    def _():
        kbuf[...] = y
        kb_ref[...] = y.astype(BF16)
        for cp in head_copies(kbuf, k4_ref, 0, i):
            cp.start()

    @pl.when(j == 2)
    def _():
        vbuf[...] = y
        vb_ref[...] = y.astype(BF16)
        for cp in head_copies(vbuf, v4_ref, 1, i):
            cp.start()
        for cp in head_copies(kbuf, k4_ref, 0, i):
            cp.wait()

        @pl.when(i == pl.num_programs(0) - 1)
        def _():
            for cp in head_copies(vbuf, v4_ref, 1, i):
                cp.wait()
            if n_prev:
                for cp in carry_copies():
                    cp.wait()


def _proj_ab(x, g, w3, wf, bf, tm, prev):
    t = x.shape[0]
    with_forget = wf is not None
    n_prev = 0 if prev is None else prev[0].shape[0]
    row = pl.BlockSpec((tm, D_MODEL), lambda i, j: (i, 0))
    hbm = pl.BlockSpec(memory_space=pl.ANY)
    in_specs = [row, pl.BlockSpec((1, D_MODEL), lambda i, j: (0, 0)),
                pl.BlockSpec((1, D_MODEL, D_MODEL), lambda i, j: (j, 0, 0))]
    args = [x, g.reshape(1, D_MODEL), w3]
    out_specs = [row, row, row]
    out_shape = [jax.ShapeDtypeStruct((t, D_MODEL), BF16)] * 3
    if with_forget:
        in_specs += [pl.BlockSpec((D_MODEL, LANES), lambda i, j: (0, 0)),
                     pl.BlockSpec((1, LANES), lambda i, j: (0, 0))]
        args += [wf, bf]
        out_specs.append(pl.BlockSpec((tm, N_HEADS), lambda i, j: (i, 0)))
        out_shape.append(jax.ShapeDtypeStruct((t, N_HEADS), F32))
    if n_prev:
        in_specs += [hbm, hbm]
        args += list(prev)
    out_specs += [hbm, hbm]
    out_shape += [jax.ShapeDtypeStruct((n_prev + 1, t, N_HEADS, HEAD_DIM), F32)] * 2
    return pl.pallas_call(
        functools.partial(_proj_ab_kernel, with_forget=with_forget, n_prev=n_prev),
        grid=(t // tm, 3),
        in_specs=in_specs,
        out_specs=out_specs,
        out_shape=out_shape,
        scratch_shapes=[pltpu.VMEM((tm, D_MODEL), BF16), pltpu.VMEM((tm, D_MODEL), F32),
                        pltpu.VMEM((tm, D_MODEL), F32), pltpu.SemaphoreType.DMA((2, N_HEADS)),
                        pltpu.SemaphoreType.DMA((2,))],
        compiler_params=_cparams(2),
        name="proj_a" if with_forget else "proj_b",
    )(*args)


def _rope_cols(y, cos, sa, sb):
    outs = []
    for c in range(y.shape[1] // LANES):
        blk = y[:, c * LANES:(c + 1) * LANES]
        nxt = pltpu.roll(blk, LANES - ROPE_DIM // 2, axis=1)
        prv = pltpu.roll(blk, ROPE_DIM // 2, axis=1)
        outs.append(blk * cos + nxt * sa + prv * sb)
    return outs


def _proj_c_kernel(x_ref, g_ref, wq_ref, wk_ref, wv_ref, cos_ref, sa_ref, sb_ref,
                   q_ref, k_ref, v_ref):
    xn = _rms(x_ref[...], g_ref[...]).astype(BF16)
    cos, sa, sb = cos_ref[...], sa_ref[...], sb_ref[...]
    q = _dot(xn, wq_ref[...])
    for c, blk in enumerate(_rope_cols(q, cos, sa, sb)):
        q_ref[:, c * LANES:(c + 1) * LANES] = blk.astype(BF16)
    k = _dot(xn, wk_ref[...])
    for c, blk in enumerate(_rope_cols(k, cos, sa, sb)):
        k_ref[:, c * LANES:(c + 1) * LANES] = blk
    v_ref[...] = _dot(xn, wv_ref[...])


def _proj_c(x, g, wq, wk2, wv2, cos, sa, sb, tm, rows_per_seq):
    t = x.shape[0]
    kvw = C_KV_HEADS * LANES
    nseq = rows_per_seq // tm
    row = lambda w: pl.BlockSpec((tm, w), lambda i: (i, 0))
    const = lambda a: pl.BlockSpec(a.shape, lambda i: (0, 0))
    tab = pl.BlockSpec((tm, LANES), lambda i: (i % nseq, 0))
    return pl.pallas_call(
        _proj_c_kernel,
        grid=(t // tm,),
        in_specs=[row(D_MODEL), pl.BlockSpec((1, D_MODEL), lambda i: (0, 0)),
                  const(wq), const(wk2), const(wv2), tab, tab, tab],
        out_specs=[row(D_MODEL), row(kvw), row(kvw)],
        out_shape=[jax.ShapeDtypeStruct((t, D_MODEL), BF16),
                   jax.ShapeDtypeStruct((t, kvw), F32),
                   jax.ShapeDtypeStruct((t, kvw), F32)],
        compiler_params=_cparams(1),
        name="proj_c",
    )(x, g.reshape(1, D_MODEL), wq, wk2, wv2, cos, sa, sb)


def _post_kernel(x_ref, o_ref, w_ref, g_ref, out_ref):
    y = _dot(o_ref[...], w_ref[...])
    out_ref[...] = x_ref[...] + _rms(y, g_ref[...])


def _post(x, o, w, g, tm):
    t = x.shape[0]
    row = pl.BlockSpec((tm, D_MODEL), lambda i: (i, 0))
    return pl.pallas_call(
        _post_kernel,
        grid=(t // tm,),
        in_specs=[row, row, pl.BlockSpec((D_MODEL, D_MODEL), lambda i: (0, 0)),
                  pl.BlockSpec((1, D_MODEL), lambda i: (0, 0))],
        out_specs=row,
        out_shape=jax.ShapeDtypeStruct((t, D_MODEL), F32),
        compiler_params=_cparams(1),
        name="mixer_out",
    )(x, o, w, g.reshape(1, D_MODEL))


def _cumsum_kernel(x_ref, o_ref):
    rows, length = x_ref.shape
    r = lax.broadcasted_iota(jnp.int32, (LANES, LANES), 0)
    c = lax.broadcasted_iota(jnp.int32, (LANES, LANES), 1)
    upper = (r <= c).astype(F32).astype(BF16)
    carry = jnp.zeros((rows, 1), F32)
    for b in range(length // LANES):
        x = x_ref[:, b * LANES:(b + 1) * LANES]
        hi = x.astype(BF16)
        r1 = x - hi.astype(F32)
        mid = r1.astype(BF16)
        lo = (r1 - mid.astype(F32)).astype(BF16)
        cs = _dot(hi, upper) + _dot(mid, upper) + _dot(lo, upper)
        o_ref[:, b * LANES:(b + 1) * LANES] = cs + carry
        carry = carry + cs[:, LANES - 1:LANES]


def _cumsum_lanes(x):
    return pl.pallas_call(
        _cumsum_kernel,
        out_shape=jax.ShapeDtypeStruct(x.shape, F32),
        compiler_params=pltpu.CompilerParams(vmem_limit_bytes=VMEM_LIMIT),
        name="cumsum",
    )(x)


FOX_HEADS = 2


def _cache_head_copies(kc_ref, vc_ref, kf_ref, vf_ref, sem, layer, b, head0, n_heads):
    copies = []
    for h in range(n_heads):
        copies.append(pltpu.make_async_copy(kc_ref.at[layer, b, :, head0 + h, :], kf_ref.at[h],
                                            sem.at[0, h]))
        copies.append(pltpu.make_async_copy(vc_ref.at[layer, b, :, head0 + h, :], vf_ref.at[h],
                                            sem.at[1, h]))
    return copies


def _fox_kernel(*refs, tq, tk, has_tail, cache_layer):
    hd = HEAD_DIM
    heads = range(FOX_HEADS)
    cols = [slice(h * hd, (h + 1) * hd) for h in heads]
    if has_tail:
        (q_ref, kc_ref, vc_ref, c_ref, kt_ref, vt_ref, ct_ref, o_ref, kb_ref, vb_ref, c2_ref,
         kf_ref, vf_ref, sem) = refs
        sm = kc_ref.shape[2]
        copies = _cache_head_copies(kc_ref, vc_ref, kf_ref, vf_ref, sem, cache_layer,
                                    pl.program_id(0), pl.program_id(1) * FOX_HEADS, FOX_HEADS)
        for cp in copies:
            cp.start()
        for cp in copies:
            cp.wait()
        k_heads = [kf_ref[h] for h in heads]
        v_heads = [vf_ref[h] for h in heads]
    else:
        q_ref, k_ref, v_ref, c_ref, o_ref, kb_ref, vb_ref, c2_ref = refs
        sm = k_ref.shape[1]
        k_heads = [k_ref[0, :, cols[h]] for h in heads]
        v_heads = [v_ref[0, :, cols[h]] for h in heads]
    sq = q_ref.shape[1]
    scale2 = hd ** -0.5 * LOG2E
    for h in heads:
        kb_ref[:, cols[h]] = k_heads[h].astype(BF16)
        vb_ref[h, :, :hd] = v_heads[h].astype(BF16)
        vb_ref[h, :, hd:] = jnp.ones((sm, hd), BF16)
        c2_ref[h] = c_ref[0, h] * LOG2E

    def step(carry, qs, ks, v1s, crows, mask):
        scores = [_dot_nt(q, k) * scale2 - crow for q, k, crow in zip(qs, ks, crows)]
        out = []
        for (m, acc), s, v1 in zip(carry, scores, v1s):
            if mask is not None:
                s = jnp.where(mask, s, -jnp.inf)
            m_new = jnp.maximum(m, jnp.max(s, axis=1, keepdims=True))
            p = jnp.exp2(s - m_new)
            out.append((m_new, jnp.exp2(m - m_new) * acc + _dot(p.astype(BF16), v1)))
        return tuple(out)

    def main_step(carry, qs, k0, width, mask):
        rows = pl.ds(k0, width)
        return step(carry, qs, [kb_ref[rows, cols[h]] for h in heads],
                    [vb_ref[h, rows, :] for h in heads], [c2_ref[h, :, rows] for h in heads], mask)

    def q_tile(qi, _):
        q0 = pl.multiple_of(qi * tq, tq)
        qs = [q_ref[0, pl.ds(q0, tq), cols[h]] for h in heads]
        carry = tuple((jnp.full((tq, 1), -jnp.inf, F32), jnp.zeros((tq, 2 * hd), F32))
                      for _ in heads)
        n_full = jnp.int32(sm // tk) if has_tail else qi
        carry = lax.fori_loop(
            0, n_full // 2,
            lambda j, cr: main_step(cr, qs, pl.multiple_of(j * 2 * tk, 2 * tk), 2 * tk, None), carry)
        carry = lax.fori_loop(
            0, n_full % 2,
            lambda _, cr: main_step(cr, qs, pl.multiple_of((n_full - 1) * tk, tk), tk, None), carry)
        if has_tail:
            nt = kt_ref.shape[1]
            row = lax.broadcasted_iota(jnp.int32, (tq, nt), 0)
            col = lax.broadcasted_iota(jnp.int32, (tq, nt), 1)
            ones = jnp.ones((nt, hd), BF16)
            carry = step(carry, qs, [kt_ref[0, :, cols[h]].astype(BF16) for h in heads],
                         [jnp.concatenate([vt_ref[0, :, cols[h]].astype(BF16), ones], axis=1)
                          for h in heads],
                         [ct_ref[0, h] * LOG2E for h in heads], col <= row)
        else:
            row = lax.broadcasted_iota(jnp.int32, (tq, tk), 0)
            col = lax.broadcasted_iota(jnp.int32, (tq, tk), 1)
            carry = main_step(carry, qs, q0, tk, col <= row)
        for h, (_, acc) in zip(heads, carry):
            o_ref[0, pl.ds(q0, tq), cols[h]] = (acc[:, :hd] / acc[:, hd:]).astype(BF16)
        return 0

    lax.fori_loop(0, sq // tq, q_tile, 0)


def _fox_attention(q, k, v, c, tail, tq, tk, cache_layer=0):
    b, sq, _ = q.shape
    has_tail = tail is not None
    sm = k.shape[2] if has_tail else k.shape[1]
    width = FOX_HEADS * HEAD_DIM
    head = lambda rows: pl.BlockSpec((1, rows, width), lambda i, h: (i, 0, h))
    crow = lambda n: pl.BlockSpec((1, FOX_HEADS, 1, n), lambda i, h: (i, h, 0, 0))
    hbm = pl.BlockSpec(memory_space=pl.ANY)
    in_specs = [head(sq), head(sm), head(sm), crow(sm)]
    args = [q, k, v, c]
    scratch = [pltpu.VMEM((sm, width), BF16), pltpu.VMEM((FOX_HEADS, sm, 2 * HEAD_DIM), BF16),
               pltpu.VMEM((FOX_HEADS, 1, sm), F32)]
    if has_tail:
        in_specs = [head(sq), hbm, hbm, crow(sm), head(sq), head(sq), crow(sq)]
        args += list(tail)
        scratch += [pltpu.VMEM((FOX_HEADS, sm, HEAD_DIM), F32),
                    pltpu.VMEM((FOX_HEADS, sm, HEAD_DIM), F32),
                    pltpu.SemaphoreType.DMA((2, FOX_HEADS))]
    return pl.pallas_call(
        functools.partial(_fox_kernel, tq=tq, tk=tk, has_tail=has_tail, cache_layer=cache_layer),
        grid=(b, N_HEADS // FOX_HEADS),
        in_specs=in_specs,
        out_specs=head(sq),
        out_shape=jax.ShapeDtypeStruct((b, sq, D_MODEL), BF16),
        scratch_shapes=scratch,
        compiler_params=_cparams(2),
        name="fox_tail" if has_tail else "fox",
    )(*args)


SB_BLOCK = 128
SB_WINDOW = 3
SB_GROUP = 4
SB_CHUNK = 512
SB_UNDERFLOW = -104.0


def _softplus(z):
    return jnp.maximum(z, 0.0) + jnp.log(1.0 + jnp.exp(-jnp.abs(z)))


def _sb_block(q, k, v, run, before, suffix, packed):
    z = _dot_nt(q, k) * (HEAD_DIM ** -0.5)
    sp = _softplus(z)
    log_fail = -sp
    if before is not None:
        log_fail = jnp.where(before, log_fail, 0.0)
    hi = log_fail.astype(BF16)
    lo = (log_fail - hi.astype(F32)).astype(BF16)
    if packed:
        later = _dot(jnp.concatenate([hi, lo], axis=1), suffix) + run
    else:
        later = _dot(hi, suffix) + _dot(lo, suffix) + run
    w = jnp.exp((z - sp) + later)
    if before is not None:
        w = jnp.where(before, w, 0.0)
    return _dot(w.astype(BF16), v), run + jnp.sum(log_fail, axis=1, keepdims=True)


def _sb_kernel(*refs, has_tail, cache_layer):
    if has_tail:
        q_ref, k_ref, v_ref, kt_ref, vt_ref, o_ref, kb_ref, vb_ref, kf_ref, vf_ref, sem = refs
    else:
        q_ref, k_ref, v_ref, o_ref, kb_ref, vb_ref = refs
    blk = SB_BLOCK
    sq = q_ref.shape[1]
    if has_tail:
        sm = k_ref.shape[2]
        per_chunk = SB_CHUNK // blk

        def load_chunk(ci):
            rows = pl.ds(pl.multiple_of(ci * SB_CHUNK, SB_CHUNK), SB_CHUNK)
            where = (cache_layer, pl.program_id(0), rows, pl.program_id(1), slice(None))
            copies = [pltpu.make_async_copy(k_ref.at[where], kf_ref, sem.at[0]),
                      pltpu.make_async_copy(v_ref.at[where], vf_ref, sem.at[1])]
            for cp in copies:
                cp.start()
            for cp in copies:
                cp.wait()
            kb_ref[rows, :] = kf_ref[...].astype(BF16)
            vb_ref[rows, :] = vf_ref[...].astype(BF16)

        load_chunk(sm // SB_CHUNK - 1)
    else:
        sm = k_ref.shape[1]
        kb_ref[...] = k_ref[0].astype(BF16)
        vb_ref[...] = v_ref[0].astype(BF16)
    r = lax.broadcasted_iota(jnp.int32, (blk, blk), 0)
    c = lax.broadcasted_iota(jnp.int32, (blk, blk), 1)
    suffix = (r > c).astype(F32).astype(BF16)
    suffix2 = jnp.concatenate([suffix, suffix], axis=0)

    def main_block(q, j, run, before):
        k0 = pl.multiple_of(j * blk, blk)
        return _sb_block(q, kb_ref[pl.ds(k0, blk), :], vb_ref[pl.ds(k0, blk), :], run, before,
                         suffix2, True)

    def finish(q, j, run, acc, rows):
        def cond(carry):
            j, run, _ = carry
            return jnp.logical_and(j >= 0, jnp.max(run) > SB_UNDERFLOW)

        def body(carry):
            j, run, acc = carry
            if has_tail:
                @pl.when(jnp.logical_and((j + 1) % per_chunk == 0, j + 1 < sm // blk))
                def _():
                    load_chunk(j // per_chunk)
            pv, run = main_block(q, j, run, None)
            return j - 1, run, acc + pv

        _, _, acc = lax.while_loop(cond, body, (j, run, acc))
        o_ref[0, rows, :] = acc.astype(BF16)

    if has_tail:
        q = q_ref[0]
        nt = kt_ref.shape[1]
        row = lax.broadcasted_iota(jnp.int32, (sq, nt), 0)
        col = lax.broadcasted_iota(jnp.int32, (sq, nt), 1)
        tr = lax.broadcasted_iota(jnp.int32, (nt, nt), 0)
        tc = lax.broadcasted_iota(jnp.int32, (nt, nt), 1)
        acc, run = _sb_block(q, kt_ref[0].astype(BF16), vt_ref[0].astype(BF16),
                             jnp.zeros((sq, 1), F32), col < row,
                             (tr > tc).astype(F32).astype(BF16), False)
        finish(q, jnp.int32(sm // blk - 1), run, acc, slice(None))
        return

    def windows(tiles, n_older):
        nb = n_older + 1
        first = [pl.multiple_of((qi - n_older) * blk, blk) for _, qi in tiles]
        zs = []
        for (q, _), k0 in zip(tiles, first):
            z = _dot_nt(q, kb_ref[pl.ds(k0, nb * blk), :]) * (HEAD_DIM ** -0.5)
            zs.append([z[:, d * blk:(d + 1) * blk] for d in range(nb)])
        pre = []
        for blocks in zs:
            items = []
            for d, z in enumerate(blocks):
                sp = _softplus(z)
                log_fail = jnp.where(c < r, -sp, 0.0) if d == nb - 1 else -sp
                hi = log_fail.astype(BF16)
                lo = (log_fail - hi.astype(F32)).astype(BF16)
                items.append((z - sp, jnp.concatenate([hi, lo], axis=1),
                              jnp.sum(log_fail, axis=1, keepdims=True)))
            pre.append(items)
        laters = [[_dot(packed, suffix2) for _, packed, _ in items] for items in pre]
        outs = []
        for items, later, k0 in zip(pre, laters, first):
            run = jnp.zeros((blk, 1), F32)
            ws = [None] * nb
            for d in reversed(range(nb)):
                log_sig, _, row_sum = items[d]
                w = jnp.exp(log_sig + (later[d] + run))
                if d == nb - 1:
                    w = jnp.where(c < r, w, 0.0)
                ws[d] = w.astype(BF16)
                run = run + row_sum
            acc = _dot(jnp.concatenate(ws, axis=1), vb_ref[pl.ds(k0, nb * blk), :])
            outs += [run, acc]
        return tuple(outs)

    def group(p, _):
        qis = [SB_GROUP * p + i for i in range(SB_GROUP)]
        rows = [pl.ds(pl.multiple_of(qi * blk, blk), blk) for qi in qis]
        tiles = [(q_ref[0, rw, :], qi) for rw, qi in zip(rows, qis)]
        wide = qis[0] >= SB_WINDOW - 1
        res = lax.cond(wide, lambda: windows(tiles, SB_WINDOW - 1), lambda: windows(tiles, 0))
        done = jnp.where(wide, SB_WINDOW - 1, 0)
        for i, ((q, qi), rw) in enumerate(zip(tiles, rows)):
            finish(q, qi - 1 - done, res[2 * i], res[2 * i + 1], rw)
        return 0

    lax.fori_loop(0, sq // (SB_GROUP * blk), group, 0)


def _sb_attention(q, k, v, tail, cache_layer=0):
    b, sq, _ = q.shape
    has_tail = tail is not None
    sm = k.shape[2] if has_tail else k.shape[1]
    assert sm % SB_BLOCK == 0 and (sm % SB_CHUNK == 0 if has_tail
                                   else (sq == sm and sq % (SB_GROUP * SB_BLOCK) == 0))
    head = lambda rows: pl.BlockSpec((1, rows, HEAD_DIM), lambda i, h: (i, 0, h))
    in_specs = [head(sq), head(sm), head(sm)]
    args = [q, k, v]
    scratch = [pltpu.VMEM((sm, HEAD_DIM), BF16), pltpu.VMEM((sm, HEAD_DIM), BF16)]
    if has_tail:
        hbm = pl.BlockSpec(memory_space=pl.ANY)
        in_specs = [head(sq), hbm, hbm, head(sq), head(sq)]
        args += list(tail)
        scratch += [pltpu.VMEM((SB_CHUNK, HEAD_DIM), F32), pltpu.VMEM((SB_CHUNK, HEAD_DIM), F32),
                    pltpu.SemaphoreType.DMA((2,))]
    return pl.pallas_call(
        functools.partial(_sb_kernel, has_tail=has_tail, cache_layer=cache_layer),
        grid=(b, N_HEADS),
        in_specs=in_specs,
        out_specs=head(sq),
        out_shape=jax.ShapeDtypeStruct((b, sq, D_MODEL), BF16),
        scratch_shapes=scratch,
        compiler_params=_cparams(2),
        name="sb_tail" if has_tail else "sb",
    )(*args)


def _swa_kernel(q_ref, k_ref, v_ref, sink_ref, o_ref, *, rq, rk, n_chunks, q_pos0, k_pos_off, k_base):
    t = pl.program_id(1)
    scale = C_HEAD_DIM ** -0.5
    lane = lax.broadcasted_iota(jnp.int32, (rq, LANES), 1)
    low = lane < C_HEAD_DIM
    rows = C_GROUP * rq

    row_pos = lax.broadcasted_iota(jnp.int32, (rows, 1), 0) % rq
    col_pos = lax.broadcasted_iota(jnp.int32, (1, rk), 1)

    def chunk(cc, _):
        c = t * n_chunks + cc
        kfirst = jnp.maximum(k_pos_off + c * rq, 0)
        krow = pl.multiple_of(kfirst - k_base, 8)
        r0 = pl.multiple_of(cc * rq, rq)
        qc = q_ref[0, pl.ds(r0, rq), :].astype(F32)
        kband = k_ref[0, pl.ds(krow, rk), :]
        vband = v_ref[0, pl.ds(krow, rk), :]
        dist = (q_pos0 + c * rq + row_pos) // CHUNK - (kfirst + col_pos) // CHUNK
        vis = (dist >= 0) & (dist <= C_WINDOW_CHUNKS)
        scores = []
        for h in range(C_KV_HEADS):
            kh = kband[:, h * LANES:(h + 1) * LANES].astype(BF16)
            parts = []
            for g in range(C_GROUP):
                col = (h * C_GROUP + g) // 2 * LANES
                pair = qc[:, col:col + LANES]
                keep = low if g % 2 == 0 else jnp.logical_not(low)
                parts.append(jnp.where(keep, pair, 0.0).astype(BF16))
            qs = jnp.concatenate(parts, axis=0)
            scores.append(jnp.where(vis, _dot_nt(qs, kh) * scale, -jnp.inf))
        probs = []
        for h, s in enumerate(scores):
            sink = sink_ref[h]
            m = jnp.maximum(jnp.max(s, axis=1, keepdims=True), sink)
            e = jnp.exp(s - m)
            inv = 1.0 / (jnp.sum(e, axis=1, keepdims=True) + jnp.exp(sink - m))
            probs.append((e.astype(BF16), inv))
        for h, (e, inv) in enumerate(probs):
            vh = vband[:, h * LANES:(h + 1) * LANES].astype(BF16)
            o = _dot(e, vh) * inv
            for pr in range(C_GROUP // 2):
                oa = o[(2 * pr) * rq:(2 * pr + 1) * rq]
                ob = o[(2 * pr + 1) * rq:(2 * pr + 2) * rq]
                col = (h * C_GROUP // 2 + pr) * LANES
                o_ref[0, pl.ds(r0, rq), col:col + LANES] = jnp.where(low, oa, ob).astype(BF16)
        return 0

    lax.fori_loop(0, n_chunks, chunk, 0)


def _swa_attention(q, k2, v2, sink_col, rq, rk, n_chunks, q_pos0, k_pos_off, k_base):
    b, sq, _ = q.shape
    sk = k2.shape[1]
    tq = rq * n_chunks
    kv = pl.BlockSpec((1, sk, C_KV_HEADS * LANES), lambda i, t: (i, 0, 0))
    qs = pl.BlockSpec((1, tq, D_MODEL), lambda i, t: (i, t, 0))
    return pl.pallas_call(
        functools.partial(_swa_kernel, rq=rq, rk=rk, n_chunks=n_chunks, q_pos0=q_pos0,
                          k_pos_off=k_pos_off, k_base=k_base),
        grid=(b, sq // tq),
        in_specs=[qs, kv, kv, pl.BlockSpec(sink_col.shape, lambda i, t: (0, 0, 0))],
        out_specs=qs,
        out_shape=jax.ShapeDtypeStruct((b, sq, D_MODEL), BF16),
        compiler_params=_cparams(2),
        name="swa",
    )(q, k2, v2, sink_col)


def _rope_tables(pos):
    half = ROPE_DIM // 2
    inv = ROPE_THETA ** (-jnp.arange(half, dtype=F32) / half)
    ang = pos.astype(F32)[:, None] * inv[None, :]
    cos, sin = jnp.cos(ang), jnp.sin(ang)
    n = pos.shape[0]
    ones = jnp.ones((n, C_HEAD_DIM - ROPE_DIM), F32)
    zeros = jnp.zeros((n, C_HEAD_DIM - ROPE_DIM), F32)
    zh = jnp.zeros((n, half), F32)
    cos64 = jnp.concatenate([cos, cos, ones], axis=1)
    sa64 = jnp.concatenate([-sin, zh, zeros], axis=1)
    sb64 = jnp.concatenate([zh, sin, zeros], axis=1)
    dup = lambda a: jnp.concatenate([a, a], axis=1)
    return dup(cos64), dup(sa64), dup(sb64)


def _dup_heads(a):
    lead = a.shape[:-1]
    a = a.reshape(lead + (C_KV_HEADS, 1, C_HEAD_DIM))
    a = jnp.broadcast_to(a, lead + (C_KV_HEADS, 2, C_HEAD_DIM))
    return a.reshape(lead + (C_KV_HEADS * LANES,))


def _undup_heads(a):
    lead = a.shape[:-1]
    return a.reshape(lead + (C_KV_HEADS, 2, C_HEAD_DIM))[..., 0, :]


def kernel(x_prompt, x_sample, cache_a_k, cache_a_v, cache_a_logf, cache_b_k, cache_b_v,
           cache_c_k, cache_c_v, norm_g, ffn_w_in, ffn_w_out, a_w_in, a_b_f, a_w_out,
           b_w_in, b_w_out, c_w_in, c_sink, c_w_out):
    depth = norm_g.shape[0]
    bp, sp, _ = x_prompt.shape
    bs, ss, _ = x_sample.shape
    p_len = cache_a_k.shape[2]
    c_buf = cache_c_k.shape[2]
    tp, ts = bp * sp, bs * ss
    tm_p, tm_s, tf = 512, ts, 512

    w_in_b = ffn_w_in.astype(BF16)
    w_out_b = ffn_w_out.astype(BF16)
    aw = D_MODEL
    a_w3 = jnp.stack([a_w_in[:, :, i * aw:(i + 1) * aw] for i in range(3)], axis=1).astype(BF16)
    a_wf = jnp.pad(a_w_in[:, :, 3 * aw:], ((0, 0), (0, 0), (0, LANES - N_HEADS))).astype(BF16)
    a_bf = jnp.pad(a_b_f, ((0, 0), (0, LANES - N_HEADS)))[:, None, :]
    b_w3 = jnp.stack([b_w_in[:, :, i * aw:(i + 1) * aw] for i in range(3)], axis=1).astype(BF16)
    kvw = C_KV_HEADS * C_HEAD_DIM
    c_wq = c_w_in[:, :, :D_MODEL].astype(BF16)
    c_wk2 = _dup_heads(c_w_in[:, :, D_MODEL:D_MODEL + kvw]).astype(BF16)
    c_wv2 = _dup_heads(c_w_in[:, :, D_MODEL + kvw:]).astype(BF16)
    a_w_out_b = a_w_out.astype(BF16)
    b_w_out_b = b_w_out.astype(BF16)
    c_w_out_b = c_w_out.astype(BF16)
    rope_p = _rope_tables(jnp.arange(sp))
    rope_s = _rope_tables(p_len + jnp.arange(ss))
    rope_s = tuple(jnp.tile(a, (bs, 1)) for a in rope_s)

    xp = x_prompt.reshape(tp, D_MODEL)
    xs = x_sample.reshape(ts, D_MODEL)
    outs = {n: [] for n in ("alp", "ckp", "cvp", "als", "cks", "cvs")}
    kv4 = {"ap": None, "as": None, "bp": None, "bs": None}

    for layer in range(depth):
        g = norm_g[layer]
        xp = _half_ffn(xp, g[0], g[1], w_in_b, w_out_b, layer, 0, tm_p, tf)
        xs = _half_ffn(xs, g[0], g[1], w_in_b, w_out_b, layer, 0, tm_s, tf)
        kind, j = layer % 3, layer // 3
        if kind == 0:
            q, k, v, lf, *kv4["ap"] = _proj_ab(xp, g[2], a_w3[j], a_wf[j], a_bf[j], tm_p,
                                               kv4["ap"])
            lft = jnp.transpose(lf.reshape(bp, sp, N_HEADS), (0, 2, 1))
            cum = _cumsum_lanes(lft.reshape(bp * N_HEADS, sp)).reshape(bp, N_HEADS, 1, sp)
            o = _fox_attention(q.reshape(bp, sp, D_MODEL), k.reshape(bp, sp, D_MODEL),
                               v.reshape(bp, sp, D_MODEL), cum, None, 512, 512)
            xp = _post(xp, o.reshape(tp, D_MODEL), a_w_out_b[j], g[3], tm_p)
            outs["alp"].append(lf.reshape(bp, sp, N_HEADS))

            q, k, v, lf, *kv4["as"] = _proj_ab(xs, g[2], a_w3[j], a_wf[j], a_bf[j], tm_s,
                                               kv4["as"])
            lf_all = jnp.concatenate([cache_a_logf[j], lf.reshape(bs, ss, N_HEADS)], axis=1)
            n_all = p_len + ss
            n_pad = -n_all % LANES
            lft = jnp.pad(jnp.transpose(lf_all, (0, 2, 1)), ((0, 0), (0, 0), (0, n_pad)))
            cum = _cumsum_lanes(lft.reshape(bs * N_HEADS, n_all + n_pad))
            cum = cum.reshape(bs, N_HEADS, 1, n_all + n_pad)
            tail = (k.reshape(bs, ss, D_MODEL), v.reshape(bs, ss, D_MODEL), cum[..., p_len:n_all])
            o = _fox_attention(q.reshape(bs, ss, D_MODEL), cache_a_k, cache_a_v, cum[..., :p_len],
                               tail, ss, 512, cache_layer=j)
            xs = _post(xs, o.reshape(ts, D_MODEL), a_w_out_b[j], g[3], tm_s)
            outs["als"].append(lf.reshape(bs, ss, N_HEADS))
        elif kind == 1:
            q, k, v, *kv4["bp"] = _proj_ab(xp, g[2], b_w3[j], None, None, tm_p, kv4["bp"])
            o = _sb_attention(q.reshape(bp, sp, D_MODEL), k.reshape(bp, sp, D_MODEL),
                              v.reshape(bp, sp, D_MODEL), None)
            xp = _post(xp, o.reshape(tp, D_MODEL), b_w_out_b[j], g[3], tm_p)

            q, k, v, *kv4["bs"] = _proj_ab(xs, g[2], b_w3[j], None, None, tm_s, kv4["bs"])
            tail = (k.reshape(bs, ss, D_MODEL), v.reshape(bs, ss, D_MODEL))
            o = _sb_attention(q.reshape(bs, ss, D_MODEL), cache_b_k, cache_b_v, tail, cache_layer=j)
            xs = _post(xs, o.reshape(ts, D_MODEL), b_w_out_b[j], g[3], tm_s)
        else:
            sink_col = lambda rq: jnp.repeat(
                c_sink[j].astype(F32).reshape(C_KV_HEADS, C_GROUP), rq, axis=1)[..., None]
            pad = C_WINDOW_CHUNKS * CHUNK
            q, k2, v2 = _proj_c(xp, g[2], c_wq[j], c_wk2[j], c_wv2[j], *rope_p, tm_p, sp)
            k2 = k2.reshape(bp, sp, -1)
            v2 = v2.reshape(bp, sp, -1)
            rq = 2 * CHUNK
            o = _swa_attention(q.reshape(bp, sp, D_MODEL), k2, v2, sink_col(rq),
                               rq, pad + rq, 4, 0, -pad, 0)
            xp = _post(xp, o.reshape(tp, D_MODEL), c_w_out_b[j], g[3], tm_p)
            buf = min(C_WINDOW, sp)
            kv_shape = (bp, buf, C_KV_HEADS, C_HEAD_DIM)
            outs["ckp"].append(_undup_heads(k2[:, sp - buf:]).reshape(kv_shape))
            outs["cvp"].append(_undup_heads(v2[:, sp - buf:]).reshape(kv_shape))

            q, k2, v2 = _proj_c(xs, g[2], c_wq[j], c_wk2[j], c_wv2[j], *rope_s, tm_s, ts)
            kk2 = jnp.concatenate([_dup_heads(cache_c_k[j].reshape(bs, c_buf, kvw)),
                                   k2.reshape(bs, ss, -1)], axis=1)
            vv2 = jnp.concatenate([_dup_heads(cache_c_v[j].reshape(bs, c_buf, kvw)),
                                   v2.reshape(bs, ss, -1)], axis=1)
            o = _swa_attention(q.reshape(bs, ss, D_MODEL), kk2, vv2, sink_col(ss),
                               ss, c_buf + ss, 1, p_len, p_len - c_buf, p_len - c_buf)
            xs = _post(xs, o.reshape(ts, D_MODEL), c_w_out_b[j], g[3], tm_s)
            kv_shape = (bs, c_buf, C_KV_HEADS, C_HEAD_DIM)
            outs["cks"].append(_undup_heads(kk2[:, -c_buf:]).reshape(kv_shape))
            outs["cvs"].append(_undup_heads(vv2[:, -c_buf:]).reshape(kv_shape))
        xp = _half_ffn(xp, g[4], g[5], w_in_b, w_out_b, layer, 1, tm_p, tf)
        xs = _half_ffn(xs, g[4], g[5], w_in_b, w_out_b, layer, 1, tm_s, tf)

    st = lambda n: jnp.stack(outs[n])
    kv = lambda n, i, b, s: kv4[n][i].reshape(-1, b, s, N_HEADS, HEAD_DIM)
    return (xp.reshape(bp, sp, D_MODEL), xs.reshape(bs, ss, D_MODEL),
            kv("ap", 0, bp, sp), kv("ap", 1, bp, sp), st("alp"),
            kv("bp", 0, bp, sp), kv("bp", 1, bp, sp), st("ckp"), st("cvp"),
            kv("as", 0, bs, ss), kv("as", 1, bs, ss), st("als"),
            kv("bs", 0, bs, ss), kv("bs", 1, bs, ss), st("cks"), st("cvs"))
```

```python
import functools
import math

import jax
import jax.numpy as jnp
import numpy as np
from jax import lax
from jax.experimental import pallas as pl
from jax.experimental.pallas import tpu as pltpu

F32 = jnp.float32
BF16 = jnp.bfloat16

D_MODEL = 2048
D_FF = 11 * D_MODEL // 4
HEAD_DIM = 128
N_HEADS = 16
C_Q_HEADS = 32
C_KV_HEADS = 4
C_GROUP = C_Q_HEADS // C_KV_HEADS
C_HEAD_DIM = 64
C_WINDOW = 128
CHUNK = 64
C_WINDOW_CHUNKS = C_WINDOW // CHUNK
ROPE_THETA = 500000.0
ROPE_DIM = C_HEAD_DIM // 4
FFN_RES = 0.5
NORM_EPS = 1e-6
LOG2E = math.log2(math.e)
LANES = 128
VMEM_LIMIT = 56 * 1024 * 1024


def _cparams(n_grid):
    return pltpu.CompilerParams(dimension_semantics=("arbitrary",) * n_grid,
                                vmem_limit_bytes=VMEM_LIMIT)


def _rms(x, g):
    ms = jnp.mean(x * x, axis=-1, keepdims=True)
    return (x * lax.rsqrt(ms + NORM_EPS)) * g


def _dot(a, b):
    return jnp.dot(a, b, preferred_element_type=F32)


def _dot_nt(a, b):
    return lax.dot_general(a, b, (((1,), (1,)), ((), ())), preferred_element_type=F32)


def _ffn_kernel(x_ref, gpre_ref, gpost_ref, wg_ref, wu_ref, wo_ref, o_ref, xn_ref, acc_ref):
    j = pl.program_id(1)

    @pl.when(j == 0)
    def _():
        xn_ref[...] = _rms(x_ref[...], gpre_ref[...]).astype(BF16)
        acc_ref[...] = jnp.zeros_like(acc_ref)

    xn = xn_ref[...]
    gate = _dot(xn, wg_ref[...])
    up = _dot(xn, wu_ref[...])
    a = (gate * jax.nn.sigmoid(gate)) * up
    acc_ref[...] += _dot(a.astype(BF16), wo_ref[...])

    @pl.when(j == pl.num_programs(1) - 1)
    def _():
        o_ref[...] = x_ref[...] + FFN_RES * _rms(acc_ref[...], gpost_ref[...])


def _half_ffn(x, g_pre, g_post, w_in, w_out, layer, half, tm, tf):
    t = x.shape[0]
    nf = D_FF // tf
    return pl.pallas_call(
        _ffn_kernel,
        grid=(t // tm, nf),
        in_specs=[
            pl.BlockSpec((tm, D_MODEL), lambda i, j: (i, 0)),
            pl.BlockSpec((1, D_MODEL), lambda i, j: (0, 0)),
            pl.BlockSpec((1, D_MODEL), lambda i, j: (0, 0)),
            pl.BlockSpec((None, None, D_MODEL, tf), lambda i, j: (layer, half, 0, j)),
            pl.BlockSpec((None, None, D_MODEL, tf), lambda i, j: (layer, half, 0, j + nf)),
            pl.BlockSpec((None, None, tf, D_MODEL), lambda i, j: (layer, half, j, 0)),
        ],
        out_specs=pl.BlockSpec((tm, D_MODEL), lambda i, j: (i, 0)),
        out_shape=jax.ShapeDtypeStruct((t, D_MODEL), F32),
        scratch_shapes=[pltpu.VMEM((tm, D_MODEL), BF16), pltpu.VMEM((tm, D_MODEL), F32)],
        compiler_params=_cparams(2),
        name="half_ffn",
    )(x, g_pre.reshape(1, D_MODEL), g_post.reshape(1, D_MODEL), w_in, w_in, w_out)


def _log_sigmoid(x):
    return jnp.minimum(x, 0.0) - jnp.log1p(jnp.exp(-jnp.abs(x)))


def _proj_ab_kernel(*refs, with_forget, aliased, slots):
    refs = list(refs)
    x_ref, g_ref, w_ref = refs[:3]
    del refs[:3]
    if with_forget:
        wf_ref, bf_ref = refs[:2]
        del refs[:2]
    if aliased:
        del refs[:2]
    q_ref, kb_ref, vb_ref = refs[:3]
    del refs[:3]
    if with_forget:
        lf_ref = refs.pop(0)
    k4_ref, v4_ref, xn_ref, kbuf, vbuf, sem = refs
    i, j = pl.program_id(0), pl.program_id(1)
    tm = x_ref.shape[0]

    def head_copies(buf, dst, which, tile):
        rows = pl.ds(pl.multiple_of(tile * tm, tm), tm)
        return [pltpu.make_async_copy(buf.at[:, pl.ds(h * HEAD_DIM, HEAD_DIM)],
                                      dst.at[s, rows, h, :], sem.at[which, n, h])
                for n, s in enumerate(slots) for h in range(N_HEADS)]

    @pl.when(j == 0)
    def _():
        xn = _rms(x_ref[...], g_ref[...]).astype(BF16)
        xn_ref[...] = xn
        if with_forget:
            f = _dot(xn, wf_ref[...]) + bf_ref[...]
            lf_ref[...] = _log_sigmoid(f)[:, :N_HEADS]

    y = _dot(xn_ref[...], w_ref[0])

    @pl.when(j == 0)
    def _():
        q_ref[...] = y.astype(BF16)

        @pl.when(i > 0)
        def _():
            for cp in head_copies(vbuf, v4_ref, 1, i - 1):
                cp.wait()

    @pl.when(j == 1)
    def _():
        kbuf[...] = y
        kb_ref[...] = y.astype(BF16)
        for cp in head_copies(kbuf, k4_ref, 0, i):
            cp.start()

    @pl.when(j == 2)
    def _():
        vbuf[...] = y
        vb_ref[...] = y.astype(BF16)
        for cp in head_copies(vbuf, v4_ref, 1, i):
            cp.start()
        for cp in head_copies(kbuf, k4_ref, 0, i):
            cp.wait()

        @pl.when(i == pl.num_programs(0) - 1)
        def _():
            for cp in head_copies(vbuf, v4_ref, 1, i):
                cp.wait()


def _proj_ab(x, g, w3, wf, bf, tm, kv4, slot, n_slots):
    t = x.shape[0]
    with_forget = wf is not None
    slots = tuple(range(slot, n_slots)) if kv4 is None else (slot,)
    row = pl.BlockSpec((tm, D_MODEL), lambda i, j: (i, 0))
    hbm = pl.BlockSpec(memory_space=pl.ANY)
    in_specs = [row, pl.BlockSpec((1, D_MODEL), lambda i, j: (0, 0)),
                pl.BlockSpec((1, D_MODEL, D_MODEL), lambda i, j: (j, 0, 0))]
    args = [x, g.reshape(1, D_MODEL), w3]
    out_specs = [row, row, row]
    out_shape = [jax.ShapeDtypeStruct((t, D_MODEL), BF16)] * 3
    if with_forget:
        in_specs += [pl.BlockSpec((D_MODEL, LANES), lambda i, j: (0, 0)),
                     pl.BlockSpec((1, LANES), lambda i, j: (0, 0))]
        args += [wf, bf]
        out_specs.append(pl.BlockSpec((tm, N_HEADS), lambda i, j: (i, 0)))
        out_shape.append(jax.ShapeDtypeStruct((t, N_HEADS), F32))
    aliases = {}
    if kv4 is not None:
        aliases = {len(args): len(out_shape), len(args) + 1: len(out_shape) + 1}
        in_specs += [hbm, hbm]
        args += list(kv4)
    out_specs += [hbm, hbm]
    out_shape += [jax.ShapeDtypeStruct((n_slots, t, N_HEADS, HEAD_DIM), F32)] * 2
    return pl.pallas_call(
        functools.partial(_proj_ab_kernel, with_forget=with_forget, aliased=bool(aliases),
                          slots=slots),
        grid=(t // tm, 3),
        in_specs=in_specs,
        out_specs=out_specs,
        out_shape=out_shape,
        input_output_aliases=aliases,
        scratch_shapes=[pltpu.VMEM((tm, D_MODEL), BF16), pltpu.VMEM((tm, D_MODEL), F32),
                        pltpu.VMEM((tm, D_MODEL), F32),
                        pltpu.SemaphoreType.DMA((2, len(slots), N_HEADS))],
        compiler_params=_cparams(2),
        name="proj_a" if with_forget else "proj_b",
    )(*args)


def _rope_cols(y, cos, sa, sb):
    outs = []
    for c in range(y.shape[1] // LANES):
        blk = y[:, c * LANES:(c + 1) * LANES]
        nxt = pltpu.roll(blk, LANES - ROPE_DIM // 2, axis=1)
        prv = pltpu.roll(blk, ROPE_DIM // 2, axis=1)
        outs.append(blk * cos + nxt * sa + prv * sb)
    return outs


def _proj_c_kernel(x_ref, g_ref, wq_ref, wk_ref, wv_ref, cos_ref, sa_ref, sb_ref,
                   q_ref, k_ref, v_ref):
    xn = _rms(x_ref[...], g_ref[...]).astype(BF16)
    cos, sa, sb = cos_ref[...], sa_ref[...], sb_ref[...]
    q = _dot(xn, wq_ref[...])
    for c, blk in enumerate(_rope_cols(q, cos, sa, sb)):
        q_ref[:, c * LANES:(c + 1) * LANES] = blk.astype(BF16)
    k = _dot(xn, wk_ref[...])
    for c, blk in enumerate(_rope_cols(k, cos, sa, sb)):
        k_ref[:, c * LANES:(c + 1) * LANES] = blk
    v_ref[...] = _dot(xn, wv_ref[...])


def _proj_c(x, g, wq, wk2, wv2, cos, sa, sb, tm, rows_per_seq):
    t = x.shape[0]
    kvw = C_KV_HEADS * LANES
    nseq = rows_per_seq // tm
    row = lambda w: pl.BlockSpec((tm, w), lambda i: (i, 0))
    const = lambda a: pl.BlockSpec(a.shape, lambda i: (0, 0))
    tab = pl.BlockSpec((tm, LANES), lambda i: (i % nseq, 0))
    return pl.pallas_call(
        _proj_c_kernel,
        grid=(t // tm,),
        in_specs=[row(D_MODEL), pl.BlockSpec((1, D_MODEL), lambda i: (0, 0)),
                  const(wq), const(wk2), const(wv2), tab, tab, tab],
        out_specs=[row(D_MODEL), row(kvw), row(kvw)],
        out_shape=[jax.ShapeDtypeStruct((t, D_MODEL), BF16),
                   jax.ShapeDtypeStruct((t, kvw), F32),
                   jax.ShapeDtypeStruct((t, kvw), F32)],
        compiler_params=_cparams(1),
        name="proj_c",
    )(x, g.reshape(1, D_MODEL), wq, wk2, wv2, cos, sa, sb)


def _post_kernel(x_ref, o_ref, w_ref, g_ref, out_ref):
    y = _dot(o_ref[...], w_ref[...])
    out_ref[...] = x_ref[...] + _rms(y, g_ref[...])


def _post(x, o, w, g, tm):
    t = x.shape[0]
    row = pl.BlockSpec((tm, D_MODEL), lambda i: (i, 0))
    return pl.pallas_call(
        _post_kernel,
        grid=(t // tm,),
        in_specs=[row, row, pl.BlockSpec((D_MODEL, D_MODEL), lambda i: (0, 0)),
                  pl.BlockSpec((1, D_MODEL), lambda i: (0, 0))],
        out_specs=row,
        out_shape=jax.ShapeDtypeStruct((t, D_MODEL), F32),
        compiler_params=_cparams(1),
        name="mixer_out",
    )(x, o, w, g.reshape(1, D_MODEL))


def _cumsum_kernel(x_ref, o_ref):
    rows, length = x_ref.shape
    r = lax.broadcasted_iota(jnp.int32, (LANES, LANES), 0)
    c = lax.broadcasted_iota(jnp.int32, (LANES, LANES), 1)
    upper = (r <= c).astype(F32).astype(BF16)
    carry = jnp.zeros((rows, 1), F32)
    for b in range(length // LANES):
        x = x_ref[:, b * LANES:(b + 1) * LANES]
        hi = x.astype(BF16)
        r1 = x - hi.astype(F32)
        mid = r1.astype(BF16)
        lo = (r1 - mid.astype(F32)).astype(BF16)
        cs = _dot(hi, upper) + _dot(mid, upper) + _dot(lo, upper)
        o_ref[:, b * LANES:(b + 1) * LANES] = cs + carry
        carry = carry + cs[:, LANES - 1:LANES]


def _cumsum_lanes(x):
    return pl.pallas_call(
        _cumsum_kernel,
        out_shape=jax.ShapeDtypeStruct(x.shape, F32),
        compiler_params=pltpu.CompilerParams(vmem_limit_bytes=VMEM_LIMIT),
        name="cumsum",
    )(x)


FOX_HEADS = 2


def _cache_head_copies(kc_ref, vc_ref, kf_ref, vf_ref, sem, layer, b, head0, n_heads):
    copies = []
    for h in range(n_heads):
        copies.append(pltpu.make_async_copy(kc_ref.at[layer, b, :, head0 + h, :], kf_ref.at[h],
                                            sem.at[0, h]))
        copies.append(pltpu.make_async_copy(vc_ref.at[layer, b, :, head0 + h, :], vf_ref.at[h],
                                            sem.at[1, h]))
    return copies


def _fox_kernel(*refs, tq, tk, has_tail, cache_layer):
    hd = HEAD_DIM
    heads = range(FOX_HEADS)
    cols = [slice(h * hd, (h + 1) * hd) for h in heads]
    if has_tail:
        (q_ref, kc_ref, vc_ref, c_ref, kt_ref, vt_ref, ct_ref, o_ref, kb_ref, vb_ref, c2_ref,
         kf_ref, vf_ref, sem) = refs
        sm = kc_ref.shape[2]
        copies = _cache_head_copies(kc_ref, vc_ref, kf_ref, vf_ref, sem, cache_layer,
                                    pl.program_id(0), pl.program_id(1) * FOX_HEADS, FOX_HEADS)
        for cp in copies:
            cp.start()
        for cp in copies:
            cp.wait()
        k_heads = [kf_ref[h] for h in heads]
        v_heads = [vf_ref[h] for h in heads]
    else:
        q_ref, k_ref, v_ref, c_ref, o_ref, kb_ref, vb_ref, c2_ref = refs
        sm = k_ref.shape[1]
        k_heads = [k_ref[0, :, cols[h]] for h in heads]
        v_heads = [v_ref[0, :, cols[h]] for h in heads]
    sq = q_ref.shape[1]
    scale2 = hd ** -0.5 * LOG2E
    for h in heads:
        kb_ref[:, cols[h]] = k_heads[h].astype(BF16)
        vb_ref[h, :, :hd] = v_heads[h].astype(BF16)
        vb_ref[h, :, hd:] = jnp.ones((sm, hd), BF16)
        c2_ref[h] = c_ref[0, h] * LOG2E

    def step(carry, qs, ks, v1s, crows, mask):
        scores = [_dot_nt(q, k) * scale2 - crow for q, k, crow in zip(qs, ks, crows)]
        out = []
        for (m, acc), s, v1 in zip(carry, scores, v1s):
            if mask is not None:
                s = jnp.where(mask, s, -jnp.inf)
            m_new = jnp.maximum(m, jnp.max(s, axis=1, keepdims=True))
            p = jnp.exp2(s - m_new)
            out.append((m_new, jnp.exp2(m - m_new) * acc + _dot(p.astype(BF16), v1)))
        return tuple(out)

    def main_step(carry, qs, k0, width, mask):
        rows = pl.ds(k0, width)
        return step(carry, qs, [kb_ref[rows, cols[h]] for h in heads],
                    [vb_ref[h, rows, :] for h in heads], [c2_ref[h, :, rows] for h in heads], mask)

    def q_tile(qi, _):
        q0 = pl.multiple_of(qi * tq, tq)
        qs = [q_ref[0, pl.ds(q0, tq), cols[h]] for h in heads]
        carry = tuple((jnp.full((tq, 1), -jnp.inf, F32), jnp.zeros((tq, 2 * hd), F32))
                      for _ in heads)
        n_full = jnp.int32(sm // tk) if has_tail else qi
        carry = lax.fori_loop(
            0, n_full // 2,
            lambda j, cr: main_step(cr, qs, pl.multiple_of(j * 2 * tk, 2 * tk), 2 * tk, None), carry)
        carry = lax.fori_loop(
            0, n_full % 2,
            lambda _, cr: main_step(cr, qs, pl.multiple_of((n_full - 1) * tk, tk), tk, None), carry)
        if has_tail:
            nt = kt_ref.shape[1]
            row = lax.broadcasted_iota(jnp.int32, (tq, nt), 0)
            col = lax.broadcasted_iota(jnp.int32, (tq, nt), 1)
            ones = jnp.ones((nt, hd), BF16)
            carry = step(carry, qs, [kt_ref[0, :, cols[h]].astype(BF16) for h in heads],
                         [jnp.concatenate([vt_ref[0, :, cols[h]].astype(BF16), ones], axis=1)
                          for h in heads],
                         [ct_ref[0, h] * LOG2E for h in heads], col <= row)
        else:
            row = lax.broadcasted_iota(jnp.int32, (tq, tk), 0)
            col = lax.broadcasted_iota(jnp.int32, (tq, tk), 1)
            carry = main_step(carry, qs, q0, tk, col <= row)
        for h, (_, acc) in zip(heads, carry):
            o_ref[0, pl.ds(q0, tq), cols[h]] = (acc[:, :hd] / acc[:, hd:]).astype(BF16)
        return 0

    lax.fori_loop(0, sq // tq, q_tile, 0)


def _fox_attention(q, k, v, c, tail, tq, tk, cache_layer=0):
    b, sq, _ = q.shape
    has_tail = tail is not None
    sm = k.shape[2] if has_tail else k.shape[1]
    width = FOX_HEADS * HEAD_DIM
    head = lambda rows: pl.BlockSpec((1, rows, width), lambda i, h: (i, 0, h))
    crow = lambda n: pl.BlockSpec((1, FOX_HEADS, 1, n), lambda i, h: (i, h, 0, 0))
    hbm = pl.BlockSpec(memory_space=pl.ANY)
    in_specs = [head(sq), head(sm), head(sm), crow(sm)]
    args = [q, k, v, c]
    scratch = [pltpu.VMEM((sm, width), BF16), pltpu.VMEM((FOX_HEADS, sm, 2 * HEAD_DIM), BF16),
               pltpu.VMEM((FOX_HEADS, 1, sm), F32)]
    if has_tail:
        in_specs = [head(sq), hbm, hbm, crow(sm), head(sq), head(sq), crow(sq)]
        args += list(tail)
        scratch += [pltpu.VMEM((FOX_HEADS, sm, HEAD_DIM), F32),
                    pltpu.VMEM((FOX_HEADS, sm, HEAD_DIM), F32),
                    pltpu.SemaphoreType.DMA((2, FOX_HEADS))]
    return pl.pallas_call(
        functools.partial(_fox_kernel, tq=tq, tk=tk, has_tail=has_tail, cache_layer=cache_layer),
        grid=(b, N_HEADS // FOX_HEADS),
        in_specs=in_specs,
        out_specs=head(sq),
        out_shape=jax.ShapeDtypeStruct((b, sq, D_MODEL), BF16),
        scratch_shapes=scratch,
        compiler_params=_cparams(2),
        name="fox_tail" if has_tail else "fox",
    )(*args)


SB_BLOCK = 128
SB_WINDOW = 3
SB_GROUP = 4
SB_CHUNK = 512
SB_UNDERFLOW = -104.0


def _softplus(z):
    return jnp.maximum(z, 0.0) + jnp.log(1.0 + jnp.exp(-jnp.abs(z)))


def _sb_block(q, k, v, run, before, suffix, packed):
    z = _dot_nt(q, k) * (HEAD_DIM ** -0.5)
    sp = _softplus(z)
    log_fail = -sp
    if before is not None:
        log_fail = jnp.where(before, log_fail, 0.0)
    hi = log_fail.astype(BF16)
    lo = (log_fail - hi.astype(F32)).astype(BF16)
    if packed:
        later = _dot(jnp.concatenate([hi, lo], axis=1), suffix) + run
    else:
        later = _dot(hi, suffix) + _dot(lo, suffix) + run
    w = jnp.exp((z - sp) + later)
    if before is not None:
        w = jnp.where(before, w, 0.0)
    return _dot(w.astype(BF16), v), run + jnp.sum(log_fail, axis=1, keepdims=True)


def _sb_kernel(*refs, has_tail, cache_layer):
    if has_tail:
        q_ref, k_ref, v_ref, kt_ref, vt_ref, o_ref, kb_ref, vb_ref, kf_ref, vf_ref, sem = refs
    else:
        q_ref, k_ref, v_ref, o_ref, kb_ref, vb_ref = refs
    blk = SB_BLOCK
    sq = q_ref.shape[1]
    if has_tail:
        sm = k_ref.shape[2]
        per_chunk = SB_CHUNK // blk

        def load_chunk(ci):
            rows = pl.ds(pl.multiple_of(ci * SB_CHUNK, SB_CHUNK), SB_CHUNK)
            where = (cache_layer, pl.program_id(0), rows, pl.program_id(1), slice(None))
            copies = [pltpu.make_async_copy(k_ref.at[where], kf_ref, sem.at[0]),
                      pltpu.make_async_copy(v_ref.at[where], vf_ref, sem.at[1])]
            for cp in copies:
                cp.start()
            for cp in copies:
                cp.wait()
            kb_ref[rows, :] = kf_ref[...].astype(BF16)
            vb_ref[rows, :] = vf_ref[...].astype(BF16)

        load_chunk(sm // SB_CHUNK - 1)
    else:
        sm = k_ref.shape[1]
        kb_ref[...] = k_ref[0].astype(BF16)
        vb_ref[...] = v_ref[0].astype(BF16)
    r = lax.broadcasted_iota(jnp.int32, (blk, blk), 0)
    c = lax.broadcasted_iota(jnp.int32, (blk, blk), 1)
    suffix = (r > c).astype(F32).astype(BF16)
    suffix2 = jnp.concatenate([suffix, suffix], axis=0)

    def main_block(q, j, run, before):
        k0 = pl.multiple_of(j * blk, blk)
        return _sb_block(q, kb_ref[pl.ds(k0, blk), :], vb_ref[pl.ds(k0, blk), :], run, before,
                         suffix2, True)

    def finish(q, j, run, acc, rows):
        def cond(carry):
            j, run, _ = carry
            return jnp.logical_and(j >= 0, jnp.max(run) > SB_UNDERFLOW)

        def body(carry):
            j, run, acc = carry
            if has_tail:
                @pl.when(jnp.logical_and((j + 1) % per_chunk == 0, j + 1 < sm // blk))
                def _():
                    load_chunk(j // per_chunk)
            pv, run = main_block(q, j, run, None)
            return j - 1, run, acc + pv

        _, _, acc = lax.while_loop(cond, body, (j, run, acc))
        o_ref[0, rows, :] = acc.astype(BF16)

    if has_tail:
        q = q_ref[0]
        nt = kt_ref.shape[1]
        row = lax.broadcasted_iota(jnp.int32, (sq, nt), 0)
        col = lax.broadcasted_iota(jnp.int32, (sq, nt), 1)
        tr = lax.broadcasted_iota(jnp.int32, (nt, nt), 0)
        tc = lax.broadcasted_iota(jnp.int32, (nt, nt), 1)
        acc, run = _sb_block(q, kt_ref[0].astype(BF16), vt_ref[0].astype(BF16),
                             jnp.zeros((sq, 1), F32), col < row,
                             (tr > tc).astype(F32).astype(BF16), False)
        finish(q, jnp.int32(sm // blk - 1), run, acc, slice(None))
        return

    def windows(tiles, n_older):
        nb = n_older + 1
        first = [pl.multiple_of((qi - n_older) * blk, blk) for _, qi in tiles]
        zs = []
        for (q, _), k0 in zip(tiles, first):
            z = _dot_nt(q, kb_ref[pl.ds(k0, nb * blk), :]) * (HEAD_DIM ** -0.5)
            zs.append([z[:, d * blk:(d + 1) * blk] for d in range(nb)])
        pre = []
        for blocks in zs:
            items = []
            for d, z in enumerate(blocks):
                sp = _softplus(z)
                log_fail = jnp.where(c < r, -sp, 0.0) if d == nb - 1 else -sp
                hi = log_fail.astype(BF16)
                lo = (log_fail - hi.astype(F32)).astype(BF16)
                items.append((z - sp, jnp.concatenate([hi, lo], axis=1),
                              jnp.sum(log_fail, axis=1, keepdims=True)))
            pre.append(items)
        laters = [[_dot(packed, suffix2) for _, packed, _ in items] for items in pre]
        outs = []
        for items, later, k0 in zip(pre, laters, first):
            run = jnp.zeros((blk, 1), F32)
            ws = [None] * nb
            for d in reversed(range(nb)):
                log_sig, _, row_sum = items[d]
                w = jnp.exp(log_sig + (later[d] + run))
                if d == nb - 1:
                    w = jnp.where(c < r, w, 0.0)
                ws[d] = w.astype(BF16)
                run = run + row_sum
            acc = _dot(jnp.concatenate(ws, axis=1), vb_ref[pl.ds(k0, nb * blk), :])
            outs += [run, acc]
        return tuple(outs)

    def group(p, _):
        qis = [SB_GROUP * p + i for i in range(SB_GROUP)]
        rows = [pl.ds(pl.multiple_of(qi * blk, blk), blk) for qi in qis]
        tiles = [(q_ref[0, rw, :], qi) for rw, qi in zip(rows, qis)]
        wide = qis[0] >= SB_WINDOW - 1
        res = lax.cond(wide, lambda: windows(tiles, SB_WINDOW - 1), lambda: windows(tiles, 0))
        done = jnp.where(wide, SB_WINDOW - 1, 0)
        for i, ((q, qi), rw) in enumerate(zip(tiles, rows)):
            finish(q, qi - 1 - done, res[2 * i], res[2 * i + 1], rw)
        return 0

    lax.fori_loop(0, sq // (SB_GROUP * blk), group, 0)


def _sb_attention(q, k, v, tail, cache_layer=0):
    b, sq, _ = q.shape
    has_tail = tail is not None
    sm = k.shape[2] if has_tail else k.shape[1]
    assert sm % SB_BLOCK == 0 and (sm % SB_CHUNK == 0 if has_tail
                                   else (sq == sm and sq % (SB_GROUP * SB_BLOCK) == 0))
    head = lambda rows: pl.BlockSpec((1, rows, HEAD_DIM), lambda i, h: (i, 0, h))
    in_specs = [head(sq), head(sm), head(sm)]
    args = [q, k, v]
    scratch = [pltpu.VMEM((sm, HEAD_DIM), BF16), pltpu.VMEM((sm, HEAD_DIM), BF16)]
    if has_tail:
        hbm = pl.BlockSpec(memory_space=pl.ANY)
        in_specs = [head(sq), hbm, hbm, head(sq), head(sq)]
        args += list(tail)
        scratch += [pltpu.VMEM((SB_CHUNK, HEAD_DIM), F32), pltpu.VMEM((SB_CHUNK, HEAD_DIM), F32),
                    pltpu.SemaphoreType.DMA((2,))]
    return pl.pallas_call(
        functools.partial(_sb_kernel, has_tail=has_tail, cache_layer=cache_layer),
        grid=(b, N_HEADS),
        in_specs=in_specs,
        out_specs=head(sq),
        out_shape=jax.ShapeDtypeStruct((b, sq, D_MODEL), BF16),
        scratch_shapes=scratch,
        compiler_params=_cparams(2),
        name="sb_tail" if has_tail else "sb",
    )(*args)


def _swa_kernel(q_ref, k_ref, v_ref, sink_ref, o_ref, *, rq, rk, n_chunks, q_pos0, k_pos_off, k_base):
    t = pl.program_id(1)
    scale = C_HEAD_DIM ** -0.5
    lane = lax.broadcasted_iota(jnp.int32, (rq, LANES), 1)
    low = lane < C_HEAD_DIM
    rows = C_GROUP * rq

    row_pos = lax.broadcasted_iota(jnp.int32, (rows, 1), 0) % rq
    col_pos = lax.broadcasted_iota(jnp.int32, (1, rk), 1)

    def chunk(cc, _):
        c = t * n_chunks + cc
        kfirst = jnp.maximum(k_pos_off + c * rq, 0)
        krow = pl.multiple_of(kfirst - k_base, 8)
        r0 = pl.multiple_of(cc * rq, rq)
        qc = q_ref[0, pl.ds(r0, rq), :].astype(F32)
        kband = k_ref[0, pl.ds(krow, rk), :]
        vband = v_ref[0, pl.ds(krow, rk), :]
        dist = (q_pos0 + c * rq + row_pos) // CHUNK - (kfirst + col_pos) // CHUNK
        vis = (dist >= 0) & (dist <= C_WINDOW_CHUNKS)
        scores = []
        for h in range(C_KV_HEADS):
            kh = kband[:, h * LANES:(h + 1) * LANES].astype(BF16)
            parts = []
            for g in range(C_GROUP):
                col = (h * C_GROUP + g) // 2 * LANES
                pair = qc[:, col:col + LANES]
                keep = low if g % 2 == 0 else jnp.logical_not(low)
                parts.append(jnp.where(keep, pair, 0.0).astype(BF16))
            qs = jnp.concatenate(parts, axis=0)
            scores.append(jnp.where(vis, _dot_nt(qs, kh) * scale, -jnp.inf))
        probs = []
        for h, s in enumerate(scores):
            sink = sink_ref[h]
            m = jnp.maximum(jnp.max(s, axis=1, keepdims=True), sink)
            e = jnp.exp(s - m)
            inv = 1.0 / (jnp.sum(e, axis=1, keepdims=True) + jnp.exp(sink - m))
            probs.append((e.astype(BF16), inv))
        for h, (e, inv) in enumerate(probs):
            vh = vband[:, h * LANES:(h + 1) * LANES].astype(BF16)
            o = _dot(e, vh) * inv
            for pr in range(C_GROUP // 2):
                oa = o[(2 * pr) * rq:(2 * pr + 1) * rq]
                ob = o[(2 * pr + 1) * rq:(2 * pr + 2) * rq]
                col = (h * C_GROUP // 2 + pr) * LANES
                o_ref[0, pl.ds(r0, rq), col:col + LANES] = jnp.where(low, oa, ob).astype(BF16)
        return 0

    lax.fori_loop(0, n_chunks, chunk, 0)


def _swa_attention(q, k2, v2, sink_col, rq, rk, n_chunks, q_pos0, k_pos_off, k_base):
    b, sq, _ = q.shape
    sk = k2.shape[1]
    tq = rq * n_chunks
    kv = pl.BlockSpec((1, sk, C_KV_HEADS * LANES), lambda i, t: (i, 0, 0))
    qs = pl.BlockSpec((1, tq, D_MODEL), lambda i, t: (i, t, 0))
    return pl.pallas_call(
        functools.partial(_swa_kernel, rq=rq, rk=rk, n_chunks=n_chunks, q_pos0=q_pos0,
                          k_pos_off=k_pos_off, k_base=k_base),
        grid=(b, sq // tq),
        in_specs=[qs, kv, kv, pl.BlockSpec(sink_col.shape, lambda i, t: (0, 0, 0))],
        out_specs=qs,
        out_shape=jax.ShapeDtypeStruct((b, sq, D_MODEL), BF16),
        compiler_params=_cparams(2),
        name="swa",
    )(q, k2, v2, sink_col)


def _rope_tables(pos):
    half = ROPE_DIM // 2
    inv = ROPE_THETA ** (-jnp.arange(half, dtype=F32) / half)
    ang = pos.astype(F32)[:, None] * inv[None, :]
    cos, sin = jnp.cos(ang), jnp.sin(ang)
    n = pos.shape[0]
    ones = jnp.ones((n, C_HEAD_DIM - ROPE_DIM), F32)
    zeros = jnp.zeros((n, C_HEAD_DIM - ROPE_DIM), F32)
    zh = jnp.zeros((n, half), F32)
    cos64 = jnp.concatenate([cos, cos, ones], axis=1)
    sa64 = jnp.concatenate([-sin, zh, zeros], axis=1)
    sb64 = jnp.concatenate([zh, sin, zeros], axis=1)
    dup = lambda a: jnp.concatenate([a, a], axis=1)
    return dup(cos64), dup(sa64), dup(sb64)


def _dup_heads(a):
    lead = a.shape[:-1]
    a = a.reshape(lead + (C_KV_HEADS, 1, C_HEAD_DIM))
    a = jnp.broadcast_to(a, lead + (C_KV_HEADS, 2, C_HEAD_DIM))
    return a.reshape(lead + (C_KV_HEADS * LANES,))


def _undup_heads(a):
    lead = a.shape[:-1]
    return a.reshape(lead + (C_KV_HEADS, 2, C_HEAD_DIM))[..., 0, :]


def kernel(x_prompt, x_sample, cache_a_k, cache_a_v, cache_a_logf, cache_b_k, cache_b_v,
           cache_c_k, cache_c_v, norm_g, ffn_w_in, ffn_w_out, a_w_in, a_b_f, a_w_out,
           b_w_in, b_w_out, c_w_in, c_sink, c_w_out):
    depth = norm_g.shape[0]
    bp, sp, _ = x_prompt.shape
    bs, ss, _ = x_sample.shape
    p_len = cache_a_k.shape[2]
    c_buf = cache_c_k.shape[2]
    tp, ts = bp * sp, bs * ss
    tm_p, tm_s, tf = 512, ts, 512

    w_in_b = ffn_w_in.astype(BF16)
    w_out_b = ffn_w_out.astype(BF16)
    aw = D_MODEL
    a_w3 = jnp.stack([a_w_in[:, :, i * aw:(i + 1) * aw] for i in range(3)], axis=1).astype(BF16)
    a_wf = jnp.pad(a_w_in[:, :, 3 * aw:], ((0, 0), (0, 0), (0, LANES - N_HEADS))).astype(BF16)
    a_bf = jnp.pad(a_b_f, ((0, 0), (0, LANES - N_HEADS)))[:, None, :]
    b_w3 = jnp.stack([b_w_in[:, :, i * aw:(i + 1) * aw] for i in range(3)], axis=1).astype(BF16)
    kvw = C_KV_HEADS * C_HEAD_DIM
    c_wq = c_w_in[:, :, :D_MODEL].astype(BF16)
    c_wk2 = _dup_heads(c_w_in[:, :, D_MODEL:D_MODEL + kvw]).astype(BF16)
    c_wv2 = _dup_heads(c_w_in[:, :, D_MODEL + kvw:]).astype(BF16)
    a_w_out_b = a_w_out.astype(BF16)
    b_w_out_b = b_w_out.astype(BF16)
    c_w_out_b = c_w_out.astype(BF16)
    rope_p = _rope_tables(jnp.arange(sp))
    rope_s = _rope_tables(p_len + jnp.arange(ss))
    rope_s = tuple(jnp.tile(a, (bs, 1)) for a in rope_s)

    xp = x_prompt.reshape(tp, D_MODEL)
    xs = x_sample.reshape(ts, D_MODEL)
    outs = {n: [] for n in ("alp", "ckp", "cvp", "als", "cks", "cvs")}
    n_a, n_b = (depth + 2) // 3, (depth + 1) // 3
    kv4 = {"ap": None, "as": None, "bp": None, "bs": None}

    for layer in range(depth):
        g = norm_g[layer]
        xp = _half_ffn(xp, g[0], g[1], w_in_b, w_out_b, layer, 0, tm_p, tf)
        xs = _half_ffn(xs, g[0], g[1], w_in_b, w_out_b, layer, 0, tm_s, tf)
        kind, j = layer % 3, layer // 3
        if kind == 0:
            q, k, v, lf, *kv4["ap"] = _proj_ab(xp, g[2], a_w3[j], a_wf[j], a_bf[j], tm_p,
                                               kv4["ap"], j, n_a)
            lft = jnp.transpose(lf.reshape(bp, sp, N_HEADS), (0, 2, 1))
            cum = _cumsum_lanes(lft.reshape(bp * N_HEADS, sp)).reshape(bp, N_HEADS, 1, sp)
            o = _fox_attention(q.reshape(bp, sp, D_MODEL), k.reshape(bp, sp, D_MODEL),
                               v.reshape(bp, sp, D_MODEL), cum, None, 512, 512)
            xp = _post(xp, o.reshape(tp, D_MODEL), a_w_out_b[j], g[3], tm_p)
            outs["alp"].append(lf.reshape(bp, sp, N_HEADS))

            q, k, v, lf, *kv4["as"] = _proj_ab(xs, g[2], a_w3[j], a_wf[j], a_bf[j], tm_s,
                                               kv4["as"], j, n_a)
            lf_all = jnp.concatenate([cache_a_logf[j], lf.reshape(bs, ss, N_HEADS)], axis=1)
            n_all = p_len + ss
            n_pad = -n_all % LANES
            lft = jnp.pad(jnp.transpose(lf_all, (0, 2, 1)), ((0, 0), (0, 0), (0, n_pad)))
            cum = _cumsum_lanes(lft.reshape(bs * N_HEADS, n_all + n_pad))
            cum = cum.reshape(bs, N_HEADS, 1, n_all + n_pad)
            tail = (k.reshape(bs, ss, D_MODEL), v.reshape(bs, ss, D_MODEL), cum[..., p_len:n_all])
            o = _fox_attention(q.reshape(bs, ss, D_MODEL), cache_a_k, cache_a_v, cum[..., :p_len],
                               tail, ss, 512, cache_layer=j)
            xs = _post(xs, o.reshape(ts, D_MODEL), a_w_out_b[j], g[3], tm_s)
            outs["als"].append(lf.reshape(bs, ss, N_HEADS))
        elif kind == 1:
            q, k, v, *kv4["bp"] = _proj_ab(xp, g[2], b_w3[j], None, None, tm_p, kv4["bp"], j, n_b)
            o = _sb_attention(q.reshape(bp, sp, D_MODEL), k.reshape(bp, sp, D_MODEL),
                              v.reshape(bp, sp, D_MODEL), None)
            xp = _post(xp, o.reshape(tp, D_MODEL), b_w_out_b[j], g[3], tm_p)

            q, k, v, *kv4["bs"] = _proj_ab(xs, g[2], b_w3[j], None, None, tm_s, kv4["bs"], j, n_b)
            tail = (k.reshape(bs, ss, D_MODEL), v.reshape(bs, ss, D_MODEL))
            o = _sb_attention(q.reshape(bs, ss, D_MODEL), cache_b_k, cache_b_v, tail, cache_layer=j)
            xs = _post(xs, o.reshape(ts, D_MODEL), b_w_out_b[j], g[3], tm_s)
        else:
            sink_col = lambda rq: jnp.repeat(
                c_sink[j].astype(F32).reshape(C_KV_HEADS, C_GROUP), rq, axis=1)[..., None]
            pad = C_WINDOW_CHUNKS * CHUNK
            q, k2, v2 = _proj_c(xp, g[2], c_wq[j], c_wk2[j], c_wv2[j], *rope_p, tm_p, sp)
            k2 = k2.reshape(bp, sp, -1)
            v2 = v2.reshape(bp, sp, -1)
            rq = 2 * CHUNK
            o = _swa_attention(q.reshape(bp, sp, D_MODEL), k2, v2, sink_col(rq),
                               rq, pad + rq, 4, 0, -pad, 0)
            xp = _post(xp, o.reshape(tp, D_MODEL), c_w_out_b[j], g[3], tm_p)
            buf = min(C_WINDOW, sp)
            kv_shape = (bp, buf, C_KV_HEADS, C_HEAD_DIM)
            outs["ckp"].append(_undup_heads(k2[:, sp - buf:]).reshape(kv_shape))
            outs["cvp"].append(_undup_heads(v2[:, sp - buf:]).reshape(kv_shape))

            q, k2, v2 = _proj_c(xs, g[2], c_wq[j], c_wk2[j], c_wv2[j], *rope_s, tm_s, ts)
            kk2 = jnp.concatenate([_dup_heads(cache_c_k[j].reshape(bs, c_buf, kvw)),
                                   k2.reshape(bs, ss, -1)], axis=1)
            vv2 = jnp.concatenate([_dup_heads(cache_c_v[j].reshape(bs, c_buf, kvw)),
                                   v2.reshape(bs, ss, -1)], axis=1)
            o = _swa_attention(q.reshape(bs, ss, D_MODEL), kk2, vv2, sink_col(ss),
                               ss, c_buf + ss, 1, p_len, p_len - c_buf, p_len - c_buf)
            xs = _post(xs, o.reshape(ts, D_MODEL), c_w_out_b[j], g[3], tm_s)
            kv_shape = (bs, c_buf, C_KV_HEADS, C_HEAD_DIM)
            outs["cks"].append(_undup_heads(kk2[:, -c_buf:]).reshape(kv_shape))
            outs["cvs"].append(_undup_heads(vv2[:, -c_buf:]).reshape(kv_shape))
        xp = _half_ffn(xp, g[4], g[5], w_in_b, w_out_b, layer, 1, tm_p, tf)
        xs = _half_ffn(xs, g[4], g[5], w_in_b, w_out_b, layer, 1, tm_s, tf)

    st = lambda n: jnp.stack(outs[n])
    kv = lambda n, i, b, s: kv4[n][i].reshape(-1, b, s, N_HEADS, HEAD_DIM)
    return (xp.reshape(bp, sp, D_MODEL), xs.reshape(bs, ss, D_MODEL),
            kv("ap", 0, bp, sp), kv("ap", 1, bp, sp), st("alp"),
            kv("bp", 0, bp, sp), kv("bp", 1, bp, sp), st("ckp"), st("cvp"),
            kv("as", 0, bs, ss), kv("as", 1, bs, ss), st("als"),
            kv("bs", 0, bs, ss), kv("bs", 1, bs, ss), st("cks"), st("cvs"))
```

```python
import functools
import math

import jax
import jax.numpy as jnp
import numpy as np
from jax import lax
from jax.experimental import pallas as pl
from jax.experimental.pallas import tpu as pltpu

F32 = jnp.float32
BF16 = jnp.bfloat16

D_MODEL = 2048
D_FF = 11 * D_MODEL // 4
HEAD_DIM = 128
N_HEADS = 16
C_Q_HEADS = 32
C_KV_HEADS = 4
C_GROUP = C_Q_HEADS // C_KV_HEADS
C_HEAD_DIM = 64
C_WINDOW = 128
CHUNK = 64
C_WINDOW_CHUNKS = C_WINDOW // CHUNK
ROPE_THETA = 500000.0
ROPE_DIM = C_HEAD_DIM // 4
FFN_RES = 0.5
NORM_EPS = 1e-6
LOG2E = math.log2(math.e)
LANES = 128
VMEM_LIMIT = 56 * 1024 * 1024
TOKEN_TILE = 512
FF_TILE = 512
ATTN_TILE = 512
NORM_CHUNKS = 2


def _norm_chunks(tm):
    return NORM_CHUNKS if tm >= TOKEN_TILE else 1


def _cparams(n_grid):
    return pltpu.CompilerParams(dimension_semantics=("arbitrary",) * n_grid,
                                vmem_limit_bytes=VMEM_LIMIT)


def _rms(x, g):
    ms = jnp.mean(x * x, axis=-1, keepdims=True)
    return (x * lax.rsqrt(ms + NORM_EPS)) * g


def _dot(a, b):
    return jnp.dot(a, b, preferred_element_type=F32)


def _dot_nt(a, b):
    return lax.dot_general(a, b, (((1,), (1,)), ((), ())), preferred_element_type=F32)


def _row_chunks(rows, chunks):
    step = rows // chunks
    return [slice(r * step, (r + 1) * step) for r in range(chunks)]


def _ffn_kernel(x_ref, gpre_ref, gpost_ref, wg_ref, wu_ref, wo_ref, o_ref, xn_ref, acc_ref, *, chunks):
    j = pl.program_id(1)
    last = pl.num_programs(1) - 1

    def slab(xn):
        gate = _dot(xn, wg_ref[...])
        up = _dot(xn, wu_ref[...])
        a = (gate * jax.nn.sigmoid(gate)) * up
        return _dot(a.astype(BF16), wo_ref[...])

    @pl.when(j == 0)
    def _():
        for rows in _row_chunks(x_ref.shape[0], chunks):
            xn = _rms(x_ref[rows, :], gpre_ref[...]).astype(BF16)
            xn_ref[rows, :] = xn
            acc_ref[rows, :] = slab(xn)

    @pl.when(jnp.logical_and(j > 0, j < last))
    def _():
        acc_ref[...] += slab(xn_ref[...])

    @pl.when(j == last)
    def _():
        for rows in _row_chunks(x_ref.shape[0], chunks):
            y = acc_ref[rows, :] + slab(xn_ref[rows, :])
            o_ref[rows, :] = x_ref[rows, :] + FFN_RES * _rms(y, gpost_ref[...])


def _half_ffn(x, g_pre, g_post, w_in, w_out, layer, half, tm, tf):
    t = x.shape[0]
    nf = D_FF // tf
    assert nf >= 2
    return pl.pallas_call(
        functools.partial(_ffn_kernel, chunks=_norm_chunks(tm)),
        grid=(t // tm, nf),
        in_specs=[
            pl.BlockSpec((tm, D_MODEL), lambda i, j: (i, 0)),
            pl.BlockSpec((1, D_MODEL), lambda i, j: (0, 0)),
            pl.BlockSpec((1, D_MODEL), lambda i, j: (0, 0)),
            pl.BlockSpec((None, None, D_MODEL, tf), lambda i, j: (layer, half, 0, j)),
            pl.BlockSpec((None, None, D_MODEL, tf), lambda i, j: (layer, half, 0, j + nf)),
            pl.BlockSpec((None, None, tf, D_MODEL), lambda i, j: (layer, half, j, 0)),
        ],
        out_specs=pl.BlockSpec((tm, D_MODEL), lambda i, j: (i, 0)),
        out_shape=jax.ShapeDtypeStruct((t, D_MODEL), F32),
        scratch_shapes=[pltpu.VMEM((tm, D_MODEL), BF16), pltpu.VMEM((tm, D_MODEL), F32)],
        compiler_params=_cparams(2),
        name="half_ffn",
    )(x, g_pre.reshape(1, D_MODEL), g_post.reshape(1, D_MODEL), w_in, w_in, w_out)


def _log_sigmoid(x):
    return jnp.minimum(x, 0.0) - jnp.log1p(jnp.exp(-jnp.abs(x)))


def _proj_ab_kernel(*refs, with_forget, aliased, slots):
    refs = list(refs)
    x_ref, g_ref, w_ref = refs[:3]
    del refs[:3]
    if with_forget:
        wf_ref, bf_ref = refs[:2]
        del refs[:2]
    if aliased:
        del refs[:2]
    q_ref, kb_ref, vb_ref = refs[:3]
    del refs[:3]
    if with_forget:
        lf_ref = refs.pop(0)
    k4_ref, v4_ref, xn_ref, kbuf, vbuf, sem = refs
    i, j = pl.program_id(0), pl.program_id(1)
    tm = x_ref.shape[0]

    def head_copies(buf, dst, which, tile):
        rows = pl.ds(pl.multiple_of(tile * tm, tm), tm)
        return [pltpu.make_async_copy(buf.at[:, pl.ds(h * HEAD_DIM, HEAD_DIM)],
                                      dst.at[s, rows, h, :], sem.at[which, n, h])
                for n, s in enumerate(slots) for h in range(N_HEADS)]

    @pl.when(j == 0)
    def _():
        for rows in _row_chunks(tm, _norm_chunks(tm)):
            xn = _rms(x_ref[rows, :], g_ref[...]).astype(BF16)
            xn_ref[rows, :] = xn
            q_ref[rows, :] = _dot(xn, w_ref[0]).astype(BF16)
            if with_forget:
                f = _dot(xn, wf_ref[...]) + bf_ref[...]
                lf_ref[rows, :] = _log_sigmoid(f)[:, :N_HEADS]

        @pl.when(i > 0)
        def _():
            for cp in head_copies(vbuf, v4_ref, 1, i - 1):
                cp.wait()

    @pl.when(j == 1)
    def _():
        y = _dot(xn_ref[...], w_ref[0])
        kbuf[...] = y
        kb_ref[...] = y.astype(BF16)
        for cp in head_copies(kbuf, k4_ref, 0, i):
            cp.start()

    @pl.when(j == 2)
    def _():
        y = _dot(xn_ref[...], w_ref[0])
        vbuf[...] = y
        vb_ref[...] = y.astype(BF16)
        for cp in head_copies(vbuf, v4_ref, 1, i):
            cp.start()
        for cp in head_copies(kbuf, k4_ref, 0, i):
            cp.wait()

        @pl.when(i == pl.num_programs(0) - 1)
        def _():
            for cp in head_copies(vbuf, v4_ref, 1, i):
                cp.wait()


def _proj_ab(x, g, w3, wf, bf, tm, kv4, slot, n_slots):
    t = x.shape[0]
    with_forget = wf is not None
    slots = tuple(range(slot, n_slots)) if kv4 is None else (slot,)
    row = pl.BlockSpec((tm, D_MODEL), lambda i, j: (i, 0))
    hbm = pl.BlockSpec(memory_space=pl.ANY)
    in_specs = [row, pl.BlockSpec((1, D_MODEL), lambda i, j: (0, 0)),
                pl.BlockSpec((1, D_MODEL, D_MODEL), lambda i, j: (j, 0, 0))]
    args = [x, g.reshape(1, D_MODEL), w3]
    out_specs = [row, row, row]
    out_shape = [jax.ShapeDtypeStruct((t, D_MODEL), BF16)] * 3
    if with_forget:
        in_specs += [pl.BlockSpec((D_MODEL, LANES), lambda i, j: (0, 0)),
                     pl.BlockSpec((1, LANES), lambda i, j: (0, 0))]
        args += [wf, bf]
        out_specs.append(pl.BlockSpec((tm, N_HEADS), lambda i, j: (i, 0)))
        out_shape.append(jax.ShapeDtypeStruct((t, N_HEADS), F32))
    aliases = {}
    if kv4 is not None:
        aliases = {len(args): len(out_shape), len(args) + 1: len(out_shape) + 1}
        in_specs += [hbm, hbm]
        args += list(kv4)
    out_specs += [hbm, hbm]
    out_shape += [jax.ShapeDtypeStruct((n_slots, t, N_HEADS, HEAD_DIM), F32)] * 2
    return pl.pallas_call(
        functools.partial(_proj_ab_kernel, with_forget=with_forget, aliased=bool(aliases),
                          slots=slots),
        grid=(t // tm, 3),
        in_specs=in_specs,
        out_specs=out_specs,
        out_shape=out_shape,
        input_output_aliases=aliases,
        scratch_shapes=[pltpu.VMEM((tm, D_MODEL), BF16), pltpu.VMEM((tm, D_MODEL), F32),
                        pltpu.VMEM((tm, D_MODEL), F32),
                        pltpu.SemaphoreType.DMA((2, len(slots), N_HEADS))],
        compiler_params=_cparams(2),
        name="proj_a" if with_forget else "proj_b",
    )(*args)


def _rope_cols(y, cos, sa, sb):
    outs = []
    for c in range(y.shape[1] // LANES):
        blk = y[:, c * LANES:(c + 1) * LANES]
        nxt = pltpu.roll(blk, LANES - ROPE_DIM // 2, axis=1)
        prv = pltpu.roll(blk, ROPE_DIM // 2, axis=1)
        outs.append(blk * cos + nxt * sa + prv * sb)
    return outs


def _proj_c_kernel(x_ref, g_ref, wq_ref, wk_ref, wv_ref, cos_ref, sa_ref, sb_ref,
                   q_ref, k_ref, v_ref):
    xn = _rms(x_ref[...], g_ref[...]).astype(BF16)
    cos, sa, sb = cos_ref[...], sa_ref[...], sb_ref[...]
    q = _dot(xn, wq_ref[...])
    for c, blk in enumerate(_rope_cols(q, cos, sa, sb)):
        q_ref[:, c * LANES:(c + 1) * LANES] = blk.astype(BF16)
    k = _dot(xn, wk_ref[...])
    for c, blk in enumerate(_rope_cols(k, cos, sa, sb)):
        k_ref[:, c * LANES:(c + 1) * LANES] = blk
    v_ref[...] = _dot(xn, wv_ref[...])


def _proj_c(x, g, wq, wk2, wv2, cos, sa, sb, tm, rows_per_seq):
    t = x.shape[0]
    kvw = C_KV_HEADS * LANES
    nseq = rows_per_seq // tm
    row = lambda w: pl.BlockSpec((tm, w), lambda i: (i, 0))
    const = lambda a: pl.BlockSpec(a.shape, lambda i: (0, 0))
    tab = pl.BlockSpec((tm, LANES), lambda i: (i % nseq, 0))
    return pl.pallas_call(
        _proj_c_kernel,
        grid=(t // tm,),
        in_specs=[row(D_MODEL), pl.BlockSpec((1, D_MODEL), lambda i: (0, 0)),
                  const(wq), const(wk2), const(wv2), tab, tab, tab],
        out_specs=[row(D_MODEL), row(kvw), row(kvw)],
        out_shape=[jax.ShapeDtypeStruct((t, D_MODEL), BF16),
                   jax.ShapeDtypeStruct((t, kvw), F32),
                   jax.ShapeDtypeStruct((t, kvw), F32)],
        compiler_params=_cparams(1),
        name="proj_c",
    )(x, g.reshape(1, D_MODEL), wq, wk2, wv2, cos, sa, sb)


def _post_kernel(x_ref, o_ref, w_ref, g_ref, out_ref):
    tm = x_ref.shape[0]
    for rows in _row_chunks(tm, _norm_chunks(tm)):
        y = _dot(o_ref[rows, :], w_ref[...])
        out_ref[rows, :] = x_ref[rows, :] + _rms(y, g_ref[...])


def _post(x, o, w, g, tm):
    t = x.shape[0]
    row = pl.BlockSpec((tm, D_MODEL), lambda i: (i, 0))
    return pl.pallas_call(
        _post_kernel,
        grid=(t // tm,),
        in_specs=[row, row, pl.BlockSpec((D_MODEL, D_MODEL), lambda i: (0, 0)),
                  pl.BlockSpec((1, D_MODEL), lambda i: (0, 0))],
        out_specs=row,
        out_shape=jax.ShapeDtypeStruct((t, D_MODEL), F32),
        compiler_params=_cparams(1),
        name="mixer_out",
    )(x, o, w, g.reshape(1, D_MODEL))


def _cumsum_kernel(x_ref, o_ref):
    rows, length = x_ref.shape
    r = lax.broadcasted_iota(jnp.int32, (LANES, LANES), 0)
    c = lax.broadcasted_iota(jnp.int32, (LANES, LANES), 1)
    upper = (r <= c).astype(F32).astype(BF16)
    carry = jnp.zeros((rows, 1), F32)
    for b in range(length // LANES):
        x = x_ref[:, b * LANES:(b + 1) * LANES]
        hi = x.astype(BF16)
        r1 = x - hi.astype(F32)
        mid = r1.astype(BF16)
        lo = (r1 - mid.astype(F32)).astype(BF16)
        cs = _dot(hi, upper) + _dot(mid, upper) + _dot(lo, upper)
        o_ref[:, b * LANES:(b + 1) * LANES] = cs + carry
        carry = carry + cs[:, LANES - 1:LANES]


def _cumsum_lanes(x):
    return pl.pallas_call(
        _cumsum_kernel,
        out_shape=jax.ShapeDtypeStruct(x.shape, F32),
        compiler_params=pltpu.CompilerParams(vmem_limit_bytes=VMEM_LIMIT),
        name="cumsum",
    )(x)


FOX_HEADS = 2


def _cache_head_copies(kc_ref, vc_ref, kf_ref, vf_ref, sem, layer, b, head0, n_heads):
    copies = []
    for h in range(n_heads):
        copies.append(pltpu.make_async_copy(kc_ref.at[layer, b, :, head0 + h, :], kf_ref.at[h],
                                            sem.at[0, h]))
        copies.append(pltpu.make_async_copy(vc_ref.at[layer, b, :, head0 + h, :], vf_ref.at[h],
                                            sem.at[1, h]))
    return copies


def _fox_kernel(*refs, tq, tk, has_tail, cache_layer):
    hd = HEAD_DIM
    heads = range(FOX_HEADS)
    cols = [slice(h * hd, (h + 1) * hd) for h in heads]
    if has_tail:
        (q_ref, kc_ref, vc_ref, c_ref, kt_ref, vt_ref, ct_ref, o_ref, kb_ref, vb_ref, c2_ref,
         kf_ref, vf_ref, sem) = refs
        sm = kc_ref.shape[2]
        copies = _cache_head_copies(kc_ref, vc_ref, kf_ref, vf_ref, sem, cache_layer,
                                    pl.program_id(0), pl.program_id(1) * FOX_HEADS, FOX_HEADS)
        for cp in copies:
            cp.start()
        for cp in copies:
            cp.wait()
        k_heads = [kf_ref[h] for h in heads]
        v_heads = [vf_ref[h] for h in heads]
    else:
        q_ref, k_ref, v_ref, c_ref, o_ref, kb_ref, vb_ref, c2_ref = refs
        sm = k_ref.shape[1]
        k_heads = [k_ref[0, :, cols[h]] for h in heads]
        v_heads = [v_ref[0, :, cols[h]] for h in heads]
    sq = q_ref.shape[1]
    scale2 = hd ** -0.5 * LOG2E
    for h in heads:
        kb_ref[:, cols[h]] = k_heads[h].astype(BF16)
        vb_ref[h, :, :hd] = v_heads[h].astype(BF16)
        vb_ref[h, :, hd:] = jnp.ones((sm, hd), BF16)
        c2_ref[h] = c_ref[0, h] * LOG2E

    def step(carry, qs, ks, v1s, crows, mask):
        scores = [_dot_nt(q, k) * scale2 - crow for q, k, crow in zip(qs, ks, crows)]
        out = []
        for (m, acc), s, v1 in zip(carry, scores, v1s):
            if mask is not None:
                s = jnp.where(mask, s, -jnp.inf)
            m_new = jnp.maximum(m, jnp.max(s, axis=1, keepdims=True))
            p = jnp.exp2(s - m_new)
            out.append((m_new, jnp.exp2(m - m_new) * acc + _dot(p.astype(BF16), v1)))
        return tuple(out)

    def main_step(carry, qs, k0, width, mask):
        rows = pl.ds(k0, width)
        return step(carry, qs, [kb_ref[rows, cols[h]] for h in heads],
                    [vb_ref[h, rows, :] for h in heads], [c2_ref[h, :, rows] for h in heads], mask)

    def q_tile(qi, _):
        q0 = pl.multiple_of(qi * tq, tq)
        qs = [q_ref[0, pl.ds(q0, tq), cols[h]] for h in heads]
        carry = tuple((jnp.full((tq, 1), -jnp.inf, F32), jnp.zeros((tq, 2 * hd), F32))
                      for _ in heads)
        n_full = jnp.int32(sm // tk) if has_tail else qi
        carry = lax.fori_loop(
            0, n_full // 2,
            lambda j, cr: main_step(cr, qs, pl.multiple_of(j * 2 * tk, 2 * tk), 2 * tk, None), carry)
        carry = lax.fori_loop(
            0, n_full % 2,
            lambda _, cr: main_step(cr, qs, pl.multiple_of((n_full - 1) * tk, tk), tk, None), carry)
        if has_tail:
            nt = kt_ref.shape[1]
            row = lax.broadcasted_iota(jnp.int32, (tq, nt), 0)
            col = lax.broadcasted_iota(jnp.int32, (tq, nt), 1)
            ones = jnp.ones((nt, hd), BF16)
            carry = step(carry, qs, [kt_ref[0, :, cols[h]].astype(BF16) for h in heads],
                         [jnp.concatenate([vt_ref[0, :, cols[h]].astype(BF16), ones], axis=1)
                          for h in heads],
                         [ct_ref[0, h] * LOG2E for h in heads], col <= row)
        else:
            row = lax.broadcasted_iota(jnp.int32, (tq, tk), 0)
            col = lax.broadcasted_iota(jnp.int32, (tq, tk), 1)
            carry = main_step(carry, qs, q0, tk, col <= row)
        for h, (_, acc) in zip(heads, carry):
            o_ref[0, pl.ds(q0, tq), cols[h]] = (acc[:, :hd] / acc[:, hd:]).astype(BF16)
        return 0

    lax.fori_loop(0, sq // tq, q_tile, 0)


def _fox_attention(q, k, v, c, tail, tq, tk, cache_layer=0):
    b, sq, _ = q.shape
    has_tail = tail is not None
    sm = k.shape[2] if has_tail else k.shape[1]
    width = FOX_HEADS * HEAD_DIM
    head = lambda rows: pl.BlockSpec((1, rows, width), lambda i, h: (i, 0, h))
    crow = lambda n: pl.BlockSpec((1, FOX_HEADS, 1, n), lambda i, h: (i, h, 0, 0))
    hbm = pl.BlockSpec(memory_space=pl.ANY)
    in_specs = [head(sq), head(sm), head(sm), crow(sm)]
    args = [q, k, v, c]
    scratch = [pltpu.VMEM((sm, width), BF16), pltpu.VMEM((FOX_HEADS, sm, 2 * HEAD_DIM), BF16),
               pltpu.VMEM((FOX_HEADS, 1, sm), F32)]
    if has_tail:
        in_specs = [head(sq), hbm, hbm, crow(sm), head(sq), head(sq), crow(sq)]
        args += list(tail)
        scratch += [pltpu.VMEM((FOX_HEADS, sm, HEAD_DIM), F32),
                    pltpu.VMEM((FOX_HEADS, sm, HEAD_DIM), F32),
                    pltpu.SemaphoreType.DMA((2, FOX_HEADS))]
    return pl.pallas_call(
        functools.partial(_fox_kernel, tq=tq, tk=tk, has_tail=has_tail, cache_layer=cache_layer),
        grid=(b, N_HEADS // FOX_HEADS),
        in_specs=in_specs,
        out_specs=head(sq),
        out_shape=jax.ShapeDtypeStruct((b, sq, D_MODEL), BF16),
        scratch_shapes=scratch,
        compiler_params=_cparams(2),
        name="fox_tail" if has_tail else "fox",
    )(*args)


SB_BLOCK = 128
SB_WINDOW = 3
SB_GROUP = 4
SB_CHUNK = 512
SB_UNDERFLOW = -104.0


def _softplus(z):
    return jnp.maximum(z, 0.0) + jnp.log(1.0 + jnp.exp(-jnp.abs(z)))


def _sb_block(q, k, v, run, before, suffix, packed):
    z = _dot_nt(q, k) * (HEAD_DIM ** -0.5)
    sp = _softplus(z)
    log_fail = -sp
    if before is not None:
        log_fail = jnp.where(before, log_fail, 0.0)
    hi = log_fail.astype(BF16)
    lo = (log_fail - hi.astype(F32)).astype(BF16)
    if packed:
        later = _dot(jnp.concatenate([hi, lo], axis=1), suffix) + run
    else:
        later = _dot(hi, suffix) + _dot(lo, suffix) + run
    w = jnp.exp((z - sp) + later)
    if before is not None:
        w = jnp.where(before, w, 0.0)
    return _dot(w.astype(BF16), v), run + jnp.sum(log_fail, axis=1, keepdims=True)


def _sb_kernel(*refs, has_tail, cache_layer):
    if has_tail:
        q_ref, k_ref, v_ref, kt_ref, vt_ref, o_ref, kb_ref, vb_ref, kf_ref, vf_ref, sem = refs
    else:
        q_ref, k_ref, v_ref, o_ref, kb_ref, vb_ref = refs
    blk = SB_BLOCK
    sq = q_ref.shape[1]
    if has_tail:
        sm = k_ref.shape[2]
        per_chunk = SB_CHUNK // blk

        def load_chunk(ci):
            rows = pl.ds(pl.multiple_of(ci * SB_CHUNK, SB_CHUNK), SB_CHUNK)
            where = (cache_layer, pl.program_id(0), rows, pl.program_id(1), slice(None))
            copies = [pltpu.make_async_copy(k_ref.at[where], kf_ref, sem.at[0]),
                      pltpu.make_async_copy(v_ref.at[where], vf_ref, sem.at[1])]
            for cp in copies:
                cp.start()
            for cp in copies:
                cp.wait()
            kb_ref[rows, :] = kf_ref[...].astype(BF16)
            vb_ref[rows, :] = vf_ref[...].astype(BF16)

        load_chunk(sm // SB_CHUNK - 1)
    else:
        sm = k_ref.shape[1]
        kb_ref[...] = k_ref[0]
        vb_ref[...] = v_ref[0]
    r = lax.broadcasted_iota(jnp.int32, (blk, blk), 0)
    c = lax.broadcasted_iota(jnp.int32, (blk, blk), 1)
    suffix = (r > c).astype(F32).astype(BF16)
    suffix2 = jnp.concatenate([suffix, suffix], axis=0)

    def main_block(q, j, run, before):
        k0 = pl.multiple_of(j * blk, blk)
        return _sb_block(q, kb_ref[pl.ds(k0, blk), :], vb_ref[pl.ds(k0, blk), :], run, before,
                         suffix2, True)

    def finish(q, j, run, acc, rows):
        def cond(carry):
            j, run, _ = carry
            return jnp.logical_and(j >= 0, jnp.max(run) > SB_UNDERFLOW)

        def body(carry):
            j, run, acc = carry
            if has_tail:
                @pl.when(jnp.logical_and((j + 1) % per_chunk == 0, j + 1 < sm // blk))
                def _():
                    load_chunk(j // per_chunk)
            pv, run = main_block(q, j, run, None)
            return j - 1, run, acc + pv

        _, _, acc = lax.while_loop(cond, body, (j, run, acc))
        o_ref[0, rows, :] = acc.astype(BF16)

    if has_tail:
        q = q_ref[0]
        nt = kt_ref.shape[1]
        row = lax.broadcasted_iota(jnp.int32, (sq, nt), 0)
        col = lax.broadcasted_iota(jnp.int32, (sq, nt), 1)
        tr = lax.broadcasted_iota(jnp.int32, (nt, nt), 0)
        tc = lax.broadcasted_iota(jnp.int32, (nt, nt), 1)
        acc, run = _sb_block(q, kt_ref[0].astype(BF16), vt_ref[0].astype(BF16),
                             jnp.zeros((sq, 1), F32), col < row,
                             (tr > tc).astype(F32).astype(BF16), False)
        finish(q, jnp.int32(sm // blk - 1), run, acc, slice(None))
        return

    def windows(tiles, n_older):
        nb = n_older + 1
        first = [pl.multiple_of((qi - n_older) * blk, blk) for _, qi in tiles]
        zs = []
        for (q, _), k0 in zip(tiles, first):
            z = _dot_nt(q, kb_ref[pl.ds(k0, nb * blk), :]) * (HEAD_DIM ** -0.5)
            zs.append([z[:, d * blk:(d + 1) * blk] for d in range(nb)])
        pre = []
        for blocks in zs:
            items = []
            for d, z in enumerate(blocks):
                sp = _softplus(z)
                log_fail = jnp.where(c < r, -sp, 0.0) if d == nb - 1 else -sp
                hi = log_fail.astype(BF16)
                lo = (log_fail - hi.astype(F32)).astype(BF16)
                items.append((z - sp, jnp.concatenate([hi, lo], axis=1),
                              jnp.sum(log_fail, axis=1, keepdims=True)))
            pre.append(items)
        laters = [[_dot(packed, suffix2) for _, packed, _ in items] for items in pre]
        outs = []
        for items, later, k0 in zip(pre, laters, first):
            run = jnp.zeros((blk, 1), F32)
            ws = [None] * nb
            for d in reversed(range(nb)):
                log_sig, _, row_sum = items[d]
                w = jnp.exp(log_sig + (later[d] + run))
                if d == nb - 1:
                    w = jnp.where(c < r, w, 0.0)
                ws[d] = w.astype(BF16)
                run = run + row_sum
            acc = _dot(jnp.concatenate(ws, axis=1), vb_ref[pl.ds(k0, nb * blk), :])
            outs += [run, acc]
        return tuple(outs)

    def group(p, _):
        qis = [SB_GROUP * p + i for i in range(SB_GROUP)]
        rows = [pl.ds(pl.multiple_of(qi * blk, blk), blk) for qi in qis]
        tiles = [(q_ref[0, rw, :], qi) for rw, qi in zip(rows, qis)]
        wide = qis[0] >= SB_WINDOW - 1
        res = lax.cond(wide, lambda: windows(tiles, SB_WINDOW - 1), lambda: windows(tiles, 0))
        done = jnp.where(wide, SB_WINDOW - 1, 0)
        for i, ((q, qi), rw) in enumerate(zip(tiles, rows)):
            finish(q, qi - 1 - done, res[2 * i], res[2 * i + 1], rw)
        return 0

    lax.fori_loop(0, sq // (SB_GROUP * blk), group, 0)


def _sb_attention(q, k, v, tail, cache_layer=0):
    b, sq, _ = q.shape
    has_tail = tail is not None
    sm = k.shape[2] if has_tail else k.shape[1]
    assert sm % SB_BLOCK == 0 and (sm % SB_CHUNK == 0 if has_tail
                                   else (sq == sm and sq % (SB_GROUP * SB_BLOCK) == 0))
    head = lambda rows: pl.BlockSpec((1, rows, HEAD_DIM), lambda i, h: (i, 0, h))
    in_specs = [head(sq), head(sm), head(sm)]
    args = [q, k, v]
    scratch = [pltpu.VMEM((sm, HEAD_DIM), BF16), pltpu.VMEM((sm, HEAD_DIM), BF16)]
    if has_tail:
        hbm = pl.BlockSpec(memory_space=pl.ANY)
        in_specs = [head(sq), hbm, hbm, head(sq), head(sq)]
        args += list(tail)
        scratch += [pltpu.VMEM((SB_CHUNK, HEAD_DIM), F32), pltpu.VMEM((SB_CHUNK, HEAD_DIM), F32),
                    pltpu.SemaphoreType.DMA((2,))]
    return pl.pallas_call(
        functools.partial(_sb_kernel, has_tail=has_tail, cache_layer=cache_layer),
        grid=(b, N_HEADS),
        in_specs=in_specs,
        out_specs=head(sq),
        out_shape=jax.ShapeDtypeStruct((b, sq, D_MODEL), BF16),
        scratch_shapes=scratch,
        compiler_params=_cparams(2),
        name="sb_tail" if has_tail else "sb",
    )(*args)


def _swa_kernel(q_ref, k_ref, v_ref, sink_ref, o_ref, *, rq, rk, n_chunks, q_pos0, k_pos_off, k_base):
    t = pl.program_id(1)
    scale = C_HEAD_DIM ** -0.5
    lane = lax.broadcasted_iota(jnp.int32, (rq, LANES), 1)
    low = lane < C_HEAD_DIM
    rows = C_GROUP * rq

    row_pos = lax.broadcasted_iota(jnp.int32, (rows, 1), 0) % rq
    col_pos = lax.broadcasted_iota(jnp.int32, (1, rk), 1)

    def chunk(cc, _):
        c = t * n_chunks + cc
        kfirst = jnp.maximum(k_pos_off + c * rq, 0)
        krow = pl.multiple_of(kfirst - k_base, 8)
        r0 = pl.multiple_of(cc * rq, rq)
        qc = q_ref[0, pl.ds(r0, rq), :].astype(F32)
        kband = k_ref[0, pl.ds(krow, rk), :]
        vband = v_ref[0, pl.ds(krow, rk), :]
        dist = (q_pos0 + c * rq + row_pos) // CHUNK - (kfirst + col_pos) // CHUNK
        vis = (dist >= 0) & (dist <= C_WINDOW_CHUNKS)
        scores = []
        for h in range(C_KV_HEADS):
            kh = kband[:, h * LANES:(h + 1) * LANES].astype(BF16)
            parts = []
            for g in range(C_GROUP):
                col = (h * C_GROUP + g) // 2 * LANES
                pair = qc[:, col:col + LANES]
                keep = low if g % 2 == 0 else jnp.logical_not(low)
                parts.append(jnp.where(keep, pair, 0.0).astype(BF16))
            qs = jnp.concatenate(parts, axis=0)
            scores.append(jnp.where(vis, _dot_nt(qs, kh) * scale, -jnp.inf))
        probs = []
        for h, s in enumerate(scores):
            sink = sink_ref[h]
            m = jnp.maximum(jnp.max(s, axis=1, keepdims=True), sink)
            e = jnp.exp(s - m)
            inv = 1.0 / (jnp.sum(e, axis=1, keepdims=True) + jnp.exp(sink - m))
            probs.append((e.astype(BF16), inv))
        for h, (e, inv) in enumerate(probs):
            vh = vband[:, h * LANES:(h + 1) * LANES].astype(BF16)
            o = _dot(e, vh) * inv
            for pr in range(C_GROUP // 2):
                oa = o[(2 * pr) * rq:(2 * pr + 1) * rq]
                ob = o[(2 * pr + 1) * rq:(2 * pr + 2) * rq]
                col = (h * C_GROUP // 2 + pr) * LANES
                o_ref[0, pl.ds(r0, rq), col:col + LANES] = jnp.where(low, oa, ob).astype(BF16)
        return 0

    lax.fori_loop(0, n_chunks, chunk, 0)


def _swa_attention(q, k2, v2, sink_col, rq, rk, n_chunks, q_pos0, k_pos_off, k_base):
    b, sq, _ = q.shape
    sk = k2.shape[1]
    tq = rq * n_chunks
    kv = pl.BlockSpec((1, sk, C_KV_HEADS * LANES), lambda i, t: (i, 0, 0))
    qs = pl.BlockSpec((1, tq, D_MODEL), lambda i, t: (i, t, 0))
    return pl.pallas_call(
        functools.partial(_swa_kernel, rq=rq, rk=rk, n_chunks=n_chunks, q_pos0=q_pos0,
                          k_pos_off=k_pos_off, k_base=k_base),
        grid=(b, sq // tq),
        in_specs=[qs, kv, kv, pl.BlockSpec(sink_col.shape, lambda i, t: (0, 0, 0))],
        out_specs=qs,
        out_shape=jax.ShapeDtypeStruct((b, sq, D_MODEL), BF16),
        compiler_params=_cparams(2),
        name="swa",
    )(q, k2, v2, sink_col)


def _rope_tables(pos):
    half = ROPE_DIM // 2
    inv = ROPE_THETA ** (-jnp.arange(half, dtype=F32) / half)
    ang = pos.astype(F32)[:, None] * inv[None, :]
    cos, sin = jnp.cos(ang), jnp.sin(ang)
    n = pos.shape[0]
    ones = jnp.ones((n, C_HEAD_DIM - ROPE_DIM), F32)
    zeros = jnp.zeros((n, C_HEAD_DIM - ROPE_DIM), F32)
    zh = jnp.zeros((n, half), F32)
    cos64 = jnp.concatenate([cos, cos, ones], axis=1)
    sa64 = jnp.concatenate([-sin, zh, zeros], axis=1)
    sb64 = jnp.concatenate([zh, sin, zeros], axis=1)
    dup = lambda a: jnp.concatenate([a, a], axis=1)
    return dup(cos64), dup(sa64), dup(sb64)


def _dup_heads(a):
    lead = a.shape[:-1]
    a = a.reshape(lead + (C_KV_HEADS, 1, C_HEAD_DIM))
    a = jnp.broadcast_to(a, lead + (C_KV_HEADS, 2, C_HEAD_DIM))
    return a.reshape(lead + (C_KV_HEADS * LANES,))


def _undup_heads(a):
    lead = a.shape[:-1]
    return a.reshape(lead + (C_KV_HEADS, 2, C_HEAD_DIM))[..., 0, :]


def kernel(x_prompt, x_sample, cache_a_k, cache_a_v, cache_a_logf, cache_b_k, cache_b_v,
           cache_c_k, cache_c_v, norm_g, ffn_w_in, ffn_w_out, a_w_in, a_b_f, a_w_out,
           b_w_in, b_w_out, c_w_in, c_sink, c_w_out):
    depth = norm_g.shape[0]
    bp, sp, _ = x_prompt.shape
    bs, ss, _ = x_sample.shape
    p_len = cache_a_k.shape[2]
    c_buf = cache_c_k.shape[2]
    tp, ts = bp * sp, bs * ss
    tm_p, tm_s, tf = TOKEN_TILE, ts, FF_TILE

    w_in_b = ffn_w_in.astype(BF16)
    w_out_b = ffn_w_out.astype(BF16)
    aw = D_MODEL
    a_w3 = jnp.stack([a_w_in[:, :, i * aw:(i + 1) * aw] for i in range(3)], axis=1).astype(BF16)
    a_wf = jnp.pad(a_w_in[:, :, 3 * aw:], ((0, 0), (0, 0), (0, LANES - N_HEADS))).astype(BF16)
    a_bf = jnp.pad(a_b_f, ((0, 0), (0, LANES - N_HEADS)))[:, None, :]
    b_w3 = jnp.stack([b_w_in[:, :, i * aw:(i + 1) * aw] for i in range(3)], axis=1).astype(BF16)
    kvw = C_KV_HEADS * C_HEAD_DIM
    c_wq = c_w_in[:, :, :D_MODEL].astype(BF16)
    c_wk2 = _dup_heads(c_w_in[:, :, D_MODEL:D_MODEL + kvw]).astype(BF16)
    c_wv2 = _dup_heads(c_w_in[:, :, D_MODEL + kvw:]).astype(BF16)
    a_w_out_b = a_w_out.astype(BF16)
    b_w_out_b = b_w_out.astype(BF16)
    c_w_out_b = c_w_out.astype(BF16)
    rope_p = _rope_tables(jnp.arange(sp))
    rope_s = _rope_tables(p_len + jnp.arange(ss))
    rope_s = tuple(jnp.tile(a, (bs, 1)) for a in rope_s)

    xp = x_prompt.reshape(tp, D_MODEL)
    xs = x_sample.reshape(ts, D_MODEL)
    outs = {n: [] for n in ("alp", "ckp", "cvp", "als", "cks", "cvs")}
    n_a, n_b = (depth + 2) // 3, (depth + 1) // 3
    kv4 = {"ap": None, "as": None, "bp": None, "bs": None}

    for layer in range(depth):
        g = norm_g[layer]
        xp = _half_ffn(xp, g[0], g[1], w_in_b, w_out_b, layer, 0, tm_p, tf)
        xs = _half_ffn(xs, g[0], g[1], w_in_b, w_out_b, layer, 0, tm_s, tf)
        kind, j = layer % 3, layer // 3
        if kind == 0:
            q, k, v, lf, *kv4["ap"] = _proj_ab(xp, g[2], a_w3[j], a_wf[j], a_bf[j], tm_p,
                                               kv4["ap"], j, n_a)
            lft = jnp.transpose(lf.reshape(bp, sp, N_HEADS), (0, 2, 1))
            cum = _cumsum_lanes(lft.reshape(bp * N_HEADS, sp)).reshape(bp, N_HEADS, 1, sp)
            o = _fox_attention(q.reshape(bp, sp, D_MODEL), k.reshape(bp, sp, D_MODEL),
                               v.reshape(bp, sp, D_MODEL), cum, None, ATTN_TILE, ATTN_TILE)
            xp = _post(xp, o.reshape(tp, D_MODEL), a_w_out_b[j], g[3], tm_p)
            outs["alp"].append(lf.reshape(bp, sp, N_HEADS))

            q, k, v, lf, *kv4["as"] = _proj_ab(xs, g[2], a_w3[j], a_wf[j], a_bf[j], tm_s,
                                               kv4["as"], j, n_a)
            lf_all = jnp.concatenate([cache_a_logf[j], lf.reshape(bs, ss, N_HEADS)], axis=1)
            n_all = p_len + ss
            n_pad = -n_all % LANES
            lft = jnp.pad(jnp.transpose(lf_all, (0, 2, 1)), ((0, 0), (0, 0), (0, n_pad)))
            cum = _cumsum_lanes(lft.reshape(bs * N_HEADS, n_all + n_pad))
            cum = cum.reshape(bs, N_HEADS, 1, n_all + n_pad)
            tail = (k.reshape(bs, ss, D_MODEL), v.reshape(bs, ss, D_MODEL), cum[..., p_len:n_all])
            o = _fox_attention(q.reshape(bs, ss, D_MODEL), cache_a_k, cache_a_v, cum[..., :p_len],
                               tail, ss, ATTN_TILE, cache_layer=j)
            xs = _post(xs, o.reshape(ts, D_MODEL), a_w_out_b[j], g[3], tm_s)
            outs["als"].append(lf.reshape(bs, ss, N_HEADS))
        elif kind == 1:
            q, k, v, *kv4["bp"] = _proj_ab(xp, g[2], b_w3[j], None, None, tm_p, kv4["bp"], j, n_b)
            o = _sb_attention(q.reshape(bp, sp, D_MODEL), k.reshape(bp, sp, D_MODEL),
                              v.reshape(bp, sp, D_MODEL), None)
            xp = _post(xp, o.reshape(tp, D_MODEL), b_w_out_b[j], g[3], tm_p)

            q, k, v, *kv4["bs"] = _proj_ab(xs, g[2], b_w3[j], None, None, tm_s, kv4["bs"], j, n_b)
            tail = (k.reshape(bs, ss, D_MODEL), v.reshape(bs, ss, D_MODEL))
            o = _sb_attention(q.reshape(bs, ss, D_MODEL), cache_b_k, cache_b_v, tail, cache_layer=j)
            xs = _post(xs, o.reshape(ts, D_MODEL), b_w_out_b[j], g[3], tm_s)
        else:
            sink_col = lambda rq: jnp.repeat(
                c_sink[j].astype(F32).reshape(C_KV_HEADS, C_GROUP), rq, axis=1)[..., None]
            pad = C_WINDOW_CHUNKS * CHUNK
            q, k2, v2 = _proj_c(xp, g[2], c_wq[j], c_wk2[j], c_wv2[j], *rope_p, tm_p, sp)
            k2 = k2.reshape(bp, sp, -1)
            v2 = v2.reshape(bp, sp, -1)
            rq = 2 * CHUNK
            o = _swa_attention(q.reshape(bp, sp, D_MODEL), k2, v2, sink_col(rq),
                               rq, pad + rq, 4, 0, -pad, 0)
            xp = _post(xp, o.reshape(tp, D_MODEL), c_w_out_b[j], g[3], tm_p)
            buf = min(C_WINDOW, sp)
            kv_shape = (bp, buf, C_KV_HEADS, C_HEAD_DIM)
            outs["ckp"].append(_undup_heads(k2[:, sp - buf:]).reshape(kv_shape))
            outs["cvp"].append(_undup_heads(v2[:, sp - buf:]).reshape(kv_shape))

            q, k2, v2 = _proj_c(xs, g[2], c_wq[j], c_wk2[j], c_wv2[j], *rope_s, tm_s, ts)
            kk2 = jnp.concatenate([_dup_heads(cache_c_k[j].reshape(bs, c_buf, kvw)),
                                   k2.reshape(bs, ss, -1)], axis=1)
            vv2 = jnp.concatenate([_dup_heads(cache_c_v[j].reshape(bs, c_buf, kvw)),
                                   v2.reshape(bs, ss, -1)], axis=1)
            o = _swa_attention(q.reshape(bs, ss, D_MODEL), kk2, vv2, sink_col(ss),
                               ss, c_buf + ss, 1, p_len, p_len - c_buf, p_len - c_buf)
            xs = _post(xs, o.reshape(ts, D_MODEL), c_w_out_b[j], g[3], tm_s)
            kv_shape = (bs, c_buf, C_KV_HEADS, C_HEAD_DIM)
            outs["cks"].append(_undup_heads(kk2[:, -c_buf:]).reshape(kv_shape))
            outs["cvs"].append(_undup_heads(vv2[:, -c_buf:]).reshape(kv_shape))
        xp = _half_ffn(xp, g[4], g[5], w_in_b, w_out_b, layer, 1, tm_p, tf)
        xs = _half_ffn(xs, g[4], g[5], w_in_b, w_out_b, layer, 1, tm_s, tf)

    st = lambda n: jnp.stack(outs[n])
    kv = lambda n, i, b, s: kv4[n][i].reshape(-1, b, s, N_HEADS, HEAD_DIM)
    return (xp.reshape(bp, sp, D_MODEL), xs.reshape(bs, ss, D_MODEL),
            kv("ap", 0, bp, sp), kv("ap", 1, bp, sp), st("alp"),
            kv("bp", 0, bp, sp), kv("bp", 1, bp, sp), st("ckp"), st("cvp"),
            kv("as", 0, bs, ss), kv("as", 1, bs, ss), st("als"),
            kv("bs", 0, bs, ss), kv("bs", 1, bs, ss), st("cks"), st("cvs"))
```
